```python
import math
import jax, jax.numpy as jnp
from jax import lax
import numpy as np

D_MODEL = 1024
BATCH = 4
SEQ = 4096
DEPTH = 4
DEC_BATCH = 128
DEC_SEQ = 8
PAST_LEN = 8192
PAGE_SIZE = 128

POOL_WINDOWS = (2, 4, 8, 16)
POOL_GROUPS = 4
POOL_GROUP_DIM = D_MODEL // 8
D_POOL = POOL_GROUPS * POOL_GROUP_DIM
POOL_BUF = max(POOL_WINDOWS) - 1

HG_HEADS = 4
HG_DK = D_MODEL // 8
HG_DV = D_MODEL // 8
D_HGK = HG_HEADS * HG_DK
D_HG = HG_HEADS * HG_DV
HG_CHUNK = 64

MLA_HEADS = 8
MLA_NOPE = D_MODEL // 16
MLA_ROPE = D_MODEL // 32
MLA_V = D_MODEL // 16
Q_RANK = D_MODEL // 4
KV_RANK = D_MODEL // 8
D_MLA = MLA_HEADS * MLA_V
MLA_SCALE = (MLA_NOPE + MLA_ROPE) ** -0.5
ROPE_THETA = 10000.0
ATTN_BLOCK = 128

D_FF = 2816
CONV_W = 3

D_IN = D_POOL + 2 * D_HGK + 2 * D_HG + Q_RANK + KV_RANK + MLA_ROPE + 3 * D_MODEL
DN_ALPHA = (2 * DEPTH) ** 0.25
DN_BETA = (8 * DEPTH) ** -0.25
LN_EPS = 1e-5
RMS_EPS = 1e-6

kernel_name = 'hybrid_pool_hgrn2_mla_convffn_step'


def _split_points():
    sizes = (D_POOL, D_HGK, D_HGK, D_HG, D_HG, Q_RANK, KV_RANK, MLA_ROPE)
    pts, acc = [], 0
    for s in sizes:
        acc += s
        pts.append(acc)
    return pts


def layer_norm(x, g, b):
    xf = x.astype(jnp.float32)
    mu = jnp.mean(xf, -1, keepdims=True)
    var = jnp.mean(jnp.square(xf - mu), -1, keepdims=True)
    return ((xf - mu) * lax.rsqrt(var + LN_EPS) * g + b).astype(x.dtype)


def rms_norm(x, g):
    xf = x.astype(jnp.float32)
    return (xf * lax.rsqrt(jnp.mean(xf * xf, -1, keepdims=True) + RMS_EPS) * g).astype(x.dtype)


def rope(x, pos):
    half = MLA_ROPE // 2
    inv = ROPE_THETA ** (-jnp.arange(half, dtype=jnp.float32) / half)
    ang = pos.astype(jnp.float32)[:, None] * inv[None, :]
    shp = (1, pos.shape[0]) + (1,) * (x.ndim - 3) + (half,)
    cos, sin = jnp.cos(ang).reshape(shp), jnp.sin(ang).reshape(shp)
    xf = x.astype(jnp.float32)
    x1, x2 = xf[..., :half], xf[..., half:]
    return jnp.concatenate([x1 * cos - x2 * sin, x1 * sin + x2 * cos], -1).astype(x.dtype)


def pool_mixer(u, buf, pos, w_pool, scale):
    B, T, _ = u.shape
    ext = jnp.concatenate([buf, u], axis=1)
    cs = jnp.cumsum(ext.astype(jnp.float32), axis=1)
    cs = jnp.concatenate([jnp.zeros_like(cs[:, :1]), cs], axis=1)
    end = cs[:, POOL_BUF + 1:]
    means = []
    for gi, w in enumerate(POOL_WINDOWS):
        sl = slice(gi * POOL_GROUP_DIM, (gi + 1) * POOL_GROUP_DIM)
        start = cs[:, POOL_BUF + 1 - w: POOL_BUF + 1 - w + T, sl]
        cnt = jnp.minimum(pos + 1, w).astype(jnp.float32)[None, :, None]
        means.append((end[..., sl] - start) / cnt)
    d = (jnp.concatenate(means, -1) - u.astype(jnp.float32)).astype(u.dtype)
    d = d.reshape(B, T, POOL_GROUPS, POOL_GROUP_DIM)
    y = jnp.einsum('btgc,gcd->btgd', d, w_pool).reshape(B, T, D_POOL) * scale
    return y, ext[:, -POOL_BUF:]


def hgrn2(q, f_logit, i, state, lb):
    B, T, H, _ = q.shape
    C = math.gcd(HG_CHUNK, T)
    n = T // C
    fl = f_logit.astype(jnp.float32)
    log_f = jnp.logaddexp(jnp.log(lb), jnp.log1p(-lb) + jax.nn.log_sigmoid(fl))
    k = (1.0 - lb) * jax.nn.sigmoid(-fl)
    qa = jax.nn.silu(q.astype(jnp.float32))

    def chunks(a):
        return a.reshape(B, n, C, H, a.shape[-1]).transpose(1, 0, 3, 2, 4)

    causal = jnp.tril(jnp.ones((C, C), dtype=bool))[:, :, None]

    def step(S, blk):
        qc, kc, vc, lc = blk
        L = jnp.cumsum(lc, axis=2)
        inter = jnp.einsum('bhtk,bhkv->bhtv', qc * jnp.exp(L), S)
        diff = jnp.where(causal, L[:, :, :, None, :] - L[:, :, None, :, :], -jnp.inf)
        A = jnp.einsum('bhtk,bhsk,bhtsk->bhts', qc, kc, jnp.exp(diff))
        intra = jnp.einsum('bhts,bhsv->bhtv', A, vc)
        L_end = L[:, :, -1:, :]
        S = jnp.exp(L_end[:, :, 0, :, None]) * S + jnp.einsum('bhsk,bhsv->bhkv', kc * jnp.exp(L_end - L), vc)
        return S, inter + intra

    S_fin, o = lax.scan(step, state.astype(jnp.float32),
                        (chunks(qa), chunks(k), chunks(i.astype(jnp.float32)), chunks(log_f)))
    o = o.transpose(1, 0, 3, 2, 4).reshape(B, T, H, HG_DV)
    return o, S_fin


def mla_attend(q_lat, q_pe, ckv, kpe, q_pos, k_pos):
    s = jnp.einsum('bthc,bsc->bhts', q_lat, ckv) + jnp.einsum('bthr,bsr->bhts', q_pe, kpe)
    s = jnp.where(q_pos[:, None] >= k_pos[None, :], s.astype(jnp.float32) * MLA_SCALE, -jnp.inf)
    p = jax.nn.softmax(s, axis=-1).astype(ckv.dtype)
    return jnp.einsum('bhts,bsc->bthc', p, ckv)


def token_mixer(x, pos0, pool_buf, hg_state, past_ckv, past_kpe, lb, p):
    B, T, _ = x.shape
    z = x @ p['w_in']
    u_pool, hq, hf, hi, hgate, cq, ckv, kpe, gates = jnp.split(z, _split_points(), axis=-1)
    pos = pos0 + jnp.arange(T, dtype=jnp.int32)

    y_pool, new_pool = pool_mixer(u_pool, pool_buf, pos, p['pool_w'], p['pool_scale'])

    o_hg, new_hg = hgrn2(hq.reshape(B, T, HG_HEADS, HG_DK), hf.reshape(B, T, HG_HEADS, HG_DK),
                         hi.reshape(B, T, HG_HEADS, HG_DV), hg_state, lb.reshape(HG_HEADS, HG_DK))
    o_hg = rms_norm(o_hg, p['hg_norm'].reshape(HG_HEADS, HG_DV)).astype(x.dtype).reshape(B, T, D_HG)
    o_hg = o_hg * jax.nn.silu(hgate)

    cq = rms_norm(cq, p['q_norm'])
    q = (cq @ p['w_uq']).reshape(B, T, MLA_HEADS, MLA_NOPE + MLA_ROPE)
    q_nope, q_pe = q[..., :MLA_NOPE], rope(q[..., MLA_NOPE:], pos)
    ckv = rms_norm(ckv, p['kv_norm'])
    kpe = rope(kpe, pos)
    q_lat = jnp.einsum('bthn,chn->bthc', q_nope, p['w_uk'])
    if past_ckv is None:
        blk = math.gcd(ATTN_BLOCK, T)
        nb = T // blk

        def one_block(args):
            ql, qp, qpos = args
            return mla_attend(ql, qp, ckv, kpe, qpos, pos)

        xs = (q_lat.reshape(B, nb, blk, MLA_HEADS, KV_RANK).swapaxes(0, 1),
              q_pe.reshape(B, nb, blk, MLA_HEADS, MLA_ROPE).swapaxes(0, 1),
              pos.reshape(nb, blk))
        o_lat = lax.map(one_block, xs).swapaxes(0, 1).reshape(B, T, MLA_HEADS, KV_RANK)
    else:
        keys_ckv = jnp.concatenate([past_ckv, ckv], axis=1)
        keys_kpe = jnp.concatenate([past_kpe, kpe], axis=1)
        k_pos = jnp.arange(keys_ckv.shape[1], dtype=jnp.int32)
        o_lat = mla_attend(q_lat, q_pe, keys_ckv, keys_kpe, pos, k_pos)
    o_mla = jnp.einsum('bthc,chv->bthv', o_lat, p['w_uv']).reshape(B, T, D_MLA)

    g_pool, g_hg, g_mla = jnp.split(jax.nn.sigmoid(gates), 3, axis=-1)
    merged = (g_pool * (y_pool @ p['w_br_pool']) + g_hg * (o_hg @ p['w_br_hg'])
              + g_mla * (o_mla @ p['w_br_mla']))
    return merged @ p['w_out'], ckv, kpe, new_pool, new_hg.astype(hg_state.dtype)


def conv_ffn(x, buf, w_up, conv_w, conv_b, w_down):
    T = x.shape[1]
    up = x @ w_up
    ext = jnp.concatenate([buf, up], axis=1)
    h = conv_b + sum(ext[:, j:j + T] * conv_w[j] for j in range(CONV_W))
    g, u = jnp.split(h, 2, axis=-1)
    return (jax.nn.silu(g) * u) @ w_down, ext[:, -(CONV_W - 1):]


def trunk_layer(x, pos0, pool_buf, hg_state, conv_buf, past_ckv, past_kpe, lb, p):
    m, ckv, kpe, new_pool, new_hg = token_mixer(x, pos0, pool_buf, hg_state, past_ckv, past_kpe, lb, p)
    x = layer_norm(DN_ALPHA * x + m, p['ln1_g'], p['ln1_b'])
    f, new_conv = conv_ffn(x, conv_buf, p['ffn_w_up'], p['ffn_conv_w'], p['ffn_conv_b'], p['ffn_w_down'])
    x = layer_norm(DN_ALPHA * x + f, p['ln2_g'], p['ln2_b'])
    return x, (ckv, kpe, new_pool, new_hg, new_conv)


def setup_inputs(seed: int = 0) -> dict:
    key = jax.random.key(seed)
    ks = jax.random.split(key, 32)

    def nrm(i, shape, scale):
        return scale * jax.random.normal(ks[i], shape, jnp.float32)

    n_pages = PAST_LEN // PAGE_SIZE
    n_used = DEC_BATCH * n_pages
    n_phys = n_used + n_used // 4
    page_table = jax.random.permutation(ks[7], n_phys)[:n_used].reshape(DEC_BATCH, n_pages).astype(jnp.int32)
    return {
        'x_prompt': nrm(0, (BATCH, SEQ, D_MODEL), 1.0),
        'x_sample': nrm(1, (DEC_BATCH, DEC_SEQ, D_MODEL), 1.0),
        'cache_ckv': nrm(2, (DEPTH, n_phys, PAGE_SIZE, KV_RANK), 1.0),
        'cache_kpe': nrm(3, (DEPTH, n_phys, PAGE_SIZE, MLA_ROPE), 1.0),
        'state_pool': nrm(4, (DEPTH, DEC_BATCH, POOL_BUF, D_POOL), 1.0),
        'state_hgrn': nrm(5, (DEPTH, DEC_BATCH, HG_HEADS, HG_DK, HG_DV), 0.5),
        'state_conv': nrm(6, (DEPTH, DEC_BATCH, CONV_W - 1, 2 * D_FF), 1.0),
        'page_table': page_table,
        'w_in': nrm(8, (DEPTH, D_MODEL, D_IN), D_MODEL ** -0.5),
        'pool_w': nrm(9, (DEPTH, POOL_GROUPS, POOL_GROUP_DIM, POOL_GROUP_DIM), POOL_GROUP_DIM ** -0.5),
        'pool_scale': 1.0 + nrm(10, (DEPTH, D_POOL), 0.1),
        'hg_lower_bounds': nrm(11, (DEPTH, D_HGK), 0.5),
        'hg_norm': 1.0 + nrm(12, (DEPTH, D_HG), 0.1),
        'q_norm': 1.0 + nrm(13, (DEPTH, Q_RANK), 0.1),
        'w_uq': nrm(14, (DEPTH, Q_RANK, MLA_HEADS * (MLA_NOPE + MLA_ROPE)), Q_RANK ** -0.5),
        'kv_norm': 1.0 + nrm(15, (DEPTH, KV_RANK), 0.1),
        'w_uk': nrm(16, (DEPTH, KV_RANK, MLA_HEADS, MLA_NOPE), KV_RANK ** -0.5),
        'w_uv': nrm(17, (DEPTH, KV_RANK, MLA_HEADS, MLA_V), DN_BETA * KV_RANK ** -0.5),
        'w_br_pool': nrm(18, (DEPTH, D_POOL, D_MODEL), DN_BETA * D_POOL ** -0.5),
        'w_br_hg': nrm(19, (DEPTH, D_HG, D_MODEL), DN_BETA * D_HG ** -0.5),
        'w_br_mla': nrm(20, (DEPTH, D_MLA, D_MODEL), DN_BETA * D_MLA ** -0.5),
        'w_out': nrm(21, (DEPTH, D_MODEL, D_MODEL), DN_BETA * D_MODEL ** -0.5),
        'ln1_g': 1.0 + nrm(22, (DEPTH, D_MODEL), 0.1),
        'ln1_b': nrm(23, (DEPTH, D_MODEL), 0.02),
        'ffn_w_up': nrm(24, (DEPTH, D_MODEL, 2 * D_FF), D_MODEL ** -0.5),
        'ffn_conv_w': nrm(25, (DEPTH, CONV_W, 2 * D_FF), CONV_W ** -0.5),
        'ffn_conv_b': nrm(26, (DEPTH, 2 * D_FF), 0.02),
        'ffn_w_down': nrm(27, (DEPTH, D_FF, D_MODEL), DN_BETA * D_FF ** -0.5),
        'ln2_g': 1.0 + nrm(28, (DEPTH, D_MODEL), 0.1),
        'ln2_b': nrm(29, (DEPTH, D_MODEL), 0.02),
    }


def reference(x_prompt, x_sample, cache_ckv, cache_kpe, state_pool, state_hgrn, state_conv, page_table,
              w_in, pool_w, pool_scale, hg_lower_bounds, hg_norm, q_norm, w_uq, kv_norm, w_uk, w_uv,
              w_br_pool, w_br_hg, w_br_mla, w_out, ln1_g, ln1_b, ffn_w_up, ffn_conv_w, ffn_conv_b,
              ffn_w_down, ln2_g, ln2_b):
    B = x_prompt.shape[0]
    DB = x_sample.shape[0]
    past_len = page_table.shape[1] * cache_ckv.shape[2]
    lb_all = jnp.cumsum(jax.nn.softmax(hg_lower_bounds.astype(jnp.float32), axis=0), axis=0)
    lb_all = lb_all - lb_all[:1]
    pool0 = jnp.zeros((B, POOL_BUF, D_POOL), x_prompt.dtype)
    hg0 = jnp.zeros((B, HG_HEADS, HG_DK, HG_DV), x_prompt.dtype)
    conv0 = jnp.zeros((B, CONV_W - 1, 2 * D_FF), x_prompt.dtype)
    yp, ys = x_prompt, x_sample
    new_p = ([], [], [], [], [])
    new_s = ([], [], [], [], [])
    for l in range(DEPTH):
        p = {'w_in': w_in[l], 'pool_w': pool_w[l], 'pool_scale': pool_scale[l], 'hg_norm': hg_norm[l],
             'q_norm': q_norm[l], 'w_uq': w_uq[l], 'kv_norm': kv_norm[l], 'w_uk': w_uk[l], 'w_uv': w_uv[l],
             'w_br_pool': w_br_pool[l], 'w_br_hg': w_br_hg[l], 'w_br_mla': w_br_mla[l], 'w_out': w_out[l],
             'ln1_g': ln1_g[l], 'ln1_b': ln1_b[l], 'ffn_w_up': ffn_w_up[l], 'ffn_conv_w': ffn_conv_w[l],
             'ffn_conv_b': ffn_conv_b[l], 'ffn_w_down': ffn_w_down[l], 'ln2_g': ln2_g[l], 'ln2_b': ln2_b[l]}
        yp, st_p = trunk_layer(yp, 0, pool0, hg0, conv0, None, None, lb_all[l], p)
        past_ckv = cache_ckv[l][page_table].reshape(DB, past_len, KV_RANK)
        past_kpe = cache_kpe[l][page_table].reshape(DB, past_len, MLA_ROPE)
        ys, st_s = trunk_layer(ys, past_len, state_pool[l], state_hgrn[l], state_conv[l],
                               past_ckv, past_kpe, lb_all[l], p)
        for lst, v in zip(new_p, st_p):
            lst.append(v)
        for lst, v in zip(new_s, st_s):
            lst.append(v)
    ckv_p, kpe_p, pool_p, hg_p, conv_p = [jnp.stack(v) for v in new_p]
    ckv_s, kpe_s, pool_s, hg_s, conv_s = [jnp.stack(v) for v in new_s]
    return (yp, ys, ckv_p, kpe_p, pool_p, hg_p, conv_p, ckv_s, kpe_s, pool_s, hg_s, conv_s)
```

```python
import functools

import jax
import jax.numpy as jnp
from jax import lax
from jax.experimental import pallas as pl
from jax.experimental.pallas import tpu as pltpu

F32 = jnp.float32
BF16 = jnp.bfloat16

D_MODEL = 1024
POOL_WINDOWS = (2, 4, 8, 16)
SEG = 128
N_SEG = 4
D_BR = N_SEG * SEG
POOL_BUF = max(POOL_WINDOWS) - 1
MLA_HEADS = 8
MLA_NOPE = 64
MLA_ROPE = 32
ROPE_HALF = MLA_ROPE // 2
MLA_V = 64
Q_RANK = 256
KV_RANK = SEG
D_QK = KV_RANK + MLA_ROPE
D_FF = 2816
CONV_W = 3
MLA_SCALE = (MLA_NOPE + MLA_ROPE) ** -0.5
ROPE_THETA = 10000.0
LN_EPS = 1e-5
RMS_EPS = 1e-6

V7X_VMEM_BYTES = 64 * 1024 * 1024
V7X_LANES = 128
V7X_SUBLANES = 8
VMEM_LIMIT = V7X_VMEM_BYTES - 6 * 1024 * 1024

C_POOL, C_HQ, C_HF, C_HI, C_HG, C_CQ, C_CKV, C_K1, C_K2, C_GATE, C_END = (
    0, 512, 1024, 1536, 2048, 2560, 2816, 2944, 3072, 3200, 3200 + 3 * D_MODEL)

NEG_BIG = -1e30
HGRN_SAFE_LOG_DECAY = -60.0


def _cparams(sem):
    return pltpu.CompilerParams(dimension_semantics=sem, vmem_limit_bytes=VMEM_LIMIT)


def _const_spec(shape):
    nd = len(shape)
    return pl.BlockSpec(shape, lambda *_: (0,) * nd, pipeline_mode=pl.Buffered(1))


def _sigmoid(x):
    return 1.0 / (1.0 + jnp.exp(-x))


def _layer_norm(h, g, b):
    mu = jnp.mean(h, axis=-1, keepdims=True)
    d = h - mu
    var = jnp.mean(d * d, axis=-1, keepdims=True)
    return d * lax.rsqrt(var + LN_EPS) * g + b


def _in_proj_kernel(layer_ref, x_ref, w_ref, hglb_ref, qn_ref, kvn_ref, wq_ref, wuk_ref, cos_ref, sin_ref,
                    u_ref, qa_ref, kk_ref, lf_ref, vv_ref, sg_ref, qlat_ref, qpe_ref, ckv_ref, kpe_ref,
                    gate_ref):
    xb = x_ref[...].astype(BF16)

    def seg(lo, hi):
        return jnp.dot(xb, w_ref[:, lo:hi], preferred_element_type=F32)

    u_ref[...] = seg(C_POOL, C_HQ)

    hq = seg(C_HQ, C_HF)
    qa_ref[...] = hq * _sigmoid(hq)

    layer = layer_ref[0]
    hb = hglb_ref[...]
    e = jnp.exp(hb - jnp.max(hb, axis=0, keepdims=True))
    row = lax.broadcasted_iota(jnp.int32, hb.shape, 0)
    lb = (jnp.sum(jnp.where((row >= 1) & (row <= layer), e, 0.0), axis=0, keepdims=True)
          / jnp.sum(e, axis=0, keepdims=True))

    fl = seg(C_HF, C_HI)
    log_sig = jnp.minimum(fl, 0.0) - jnp.log1p(jnp.exp(-jnp.abs(fl)))
    a = jnp.log(lb)
    b = jnp.log1p(-lb) + log_sig
    lf_ref[...] = jnp.maximum(a, b) + jnp.log1p(jnp.exp(-jnp.abs(a - b)))
    kk_ref[...] = (1.0 - lb) * _sigmoid(-fl)

    vv_ref[...] = seg(C_HI, C_HG)
    hg = seg(C_HG, C_CQ)
    sg_ref[...] = hg * _sigmoid(hg)

    cq = seg(C_CQ, C_CKV)
    cqn = cq * lax.rsqrt(jnp.mean(cq * cq, axis=-1, keepdims=True) + RMS_EPS) * qn_ref[...]
    q = jnp.dot(cqn.astype(BF16), wq_ref[...], preferred_element_type=F32)
    cos = cos_ref[...]
    sin = sin_ref[...]
    n_nope = MLA_HEADS * MLA_NOPE
    x1 = q[:, n_nope:n_nope + V7X_LANES]
    x2 = q[:, n_nope + V7X_LANES:]
    qpe_ref[:, :V7X_LANES] = ((x1 * cos - x2 * sin) * MLA_SCALE).astype(BF16)
    qpe_ref[:, V7X_LANES:] = ((x1 * sin + x2 * cos) * MLA_SCALE).astype(BF16)
    qlat = jnp.dot(q[:, :n_nope].astype(BF16), wuk_ref[...], preferred_element_type=F32)
    qlat_ref[...] = (qlat * MLA_SCALE).astype(BF16)

    ckv = seg(C_CKV, C_K1)
    ckv_ref[...] = ckv * lax.rsqrt(jnp.mean(ckv * ckv, axis=-1, keepdims=True) + RMS_EPS) * kvn_ref[...]

    k1 = seg(C_K1, C_K2)
    k2 = seg(C_K2, C_GATE)
    lane = lax.broadcasted_iota(jnp.int32, sin.shape, 1)
    kr = k1 * cos + k2 * jnp.where(lane < ROPE_HALF, -sin, sin)
    kpe_ref[...] = kr[:, :MLA_ROPE]

    gate_ref[...] = _sigmoid(seg(C_GATE, C_END))


def _in_proj(layer, x, wa, hglb, q_norm, kv_norm, wq, wuk, cos_tab, sin_tab, tm):
    n = x.shape[0]
    tab_blocks = cos_tab.shape[0] // tm
    row = lambda w: pl.BlockSpec((tm, w), lambda i, *_: (i, 0))
    tab = pl.BlockSpec((tm, V7X_LANES), lambda i, *_: (i % tab_blocks, 0))
    widths = (D_BR,) * 6 + (MLA_HEADS * KV_RANK, MLA_HEADS * MLA_ROPE, KV_RANK, MLA_ROPE, 3 * D_MODEL)
    dtypes = (F32,) * 6 + (BF16, BF16, F32, F32, F32)
    grid_spec = pltpu.PrefetchScalarGridSpec(
        num_scalar_prefetch=1,
        grid=(n // tm,),
        in_specs=[row(D_MODEL), _const_spec(wa.shape), _const_spec(hglb.shape), _const_spec(q_norm.shape),
                  _const_spec(kv_norm.shape), _const_spec(wq.shape), _const_spec(wuk.shape), tab, tab],
        out_specs=[row(w) for w in widths],
    )
    return pl.pallas_call(
        _in_proj_kernel,
        grid_spec=grid_spec,
        out_shape=[jax.ShapeDtypeStruct((n, w), d) for w, d in zip(widths, dtypes)],
        compiler_params=_cparams(("arbitrary",)),
        name="in_proj",
    )(layer, x, wa, hglb, q_norm, kv_norm, wq, wuk, cos_tab, sin_tab)


def _pool_prompt_kernel(u_ref, w_ref, sc_ref, y_ref, ext_ref, *, tt):
    halo = 2 * V7X_SUBLANES
    j = pl.program_id(1)

    @pl.when(j == 0)
    def _():
        ext_ref[0:halo, :] = jnp.zeros((halo, D_BR), F32)

    u = u_ref[0]
    ext_ref[halo:halo + tt, :] = u
    pos = j * tt + lax.broadcasted_iota(jnp.int32, (tt, 1), 0)
    for g, win in enumerate(POOL_WINDOWS):
        c0 = g * SEG
        acc = u[:, c0:c0 + SEG]
        for d in range(1, win):
            acc = acc + ext_ref[halo - d:halo - d + tt, c0:c0 + SEG]
        cnt = jnp.minimum(pos + 1, win).astype(F32)
        diff = (acc / cnt - u[:, c0:c0 + SEG]).astype(BF16)
        y = jnp.dot(diff, w_ref[g], preferred_element_type=F32) * sc_ref[:, c0:c0 + SEG]
        y_ref[0, :, c0:c0 + SEG] = y.astype(BF16)
    ext_ref[0:halo, :] = ext_ref[tt:tt + halo, :]


def _pool_prompt(u, w_pool, scale, tt):
    b, t, _ = u.shape
    return pl.pallas_call(
        functools.partial(_pool_prompt_kernel, tt=tt),
        grid=(b, t // tt),
        in_specs=[pl.BlockSpec((1, tt, D_BR), lambda i, j: (i, j, 0)), _const_spec(w_pool.shape),
                  _const_spec(scale.shape)],
        out_specs=pl.BlockSpec((1, tt, D_BR), lambda i, j: (i, j, 0)),
        out_shape=jax.ShapeDtypeStruct((b, t, D_BR), BF16),
        scratch_shapes=[pltpu.VMEM((tt + 2 * V7X_SUBLANES, D_BR), F32)],
        compiler_params=_cparams(("arbitrary", "arbitrary")),
        name="pool_prompt",
    )(u, w_pool, scale)


def _pool_sample_kernel(ext_ref, w_ref, sc_ref, y_ref, *, t_new, past_len):
    s = ext_ref.shape[1]
    for g, win in enumerate(POOL_WINDOWS):
        c0 = g * SEG
        diffs = []
        for t in range(t_new):
            cur = ext_ref[POOL_BUF + t, :, c0:c0 + SEG]
            acc = cur
            for d in range(1, win):
                acc = acc + ext_ref[POOL_BUF + t - d, :, c0:c0 + SEG]
            cnt = float(min(past_len + t + 1, win))
            diffs.append(acc / cnt - cur)
        diff = jnp.concatenate(diffs, axis=0).astype(BF16)
        y = jnp.dot(diff, w_ref[g], preferred_element_type=F32) * sc_ref[:, c0:c0 + SEG]
        y_ref[:, :, c0:c0 + SEG] = y.reshape(t_new, s, SEG).astype(BF16)


def _pool_sample(ext_tm, w_pool, scale, t_new, past_len):
    _, s, _ = ext_tm.shape
    return pl.pallas_call(
        functools.partial(_pool_sample_kernel, t_new=t_new, past_len=past_len),
        grid=(1,),
        in_specs=[_const_spec(ext_tm.shape), _const_spec(w_pool.shape), _const_spec(scale.shape)],
        out_specs=pl.BlockSpec((t_new, s, D_BR), lambda i: (0, 0, 0)),
        out_shape=jax.ShapeDtypeStruct((t_new, s, D_BR), BF16),
        compiler_params=_cparams(("arbitrary",)),
        name="pool_sample",
    )(ext_tm, w_pool, scale)


def _hgrn_kernel(qa_ref, kk_ref, lf_ref, vv_ref, sg_ref, nw_ref, s0_ref, o_ref, sfin_ref,
                 st_ref, lsc_ref, ksc_ref, *, c, n_chunks, mm_dtype):
    j = pl.program_id(1)

    @pl.when(j == 0)
    def _():
        for h in range(N_SEG):
            st_ref[h] = s0_ref[0, h].T

    r_i = lax.broadcasted_iota(jnp.int32, (c, c), 0)
    c_i = lax.broadcasted_iota(jnp.int32, (c, c), 1)
    causal = r_i >= c_i
    tri = causal.astype(mm_dtype)

    def nt(a, b):
        return lax.dot_general(a, b, (((1,), (1,)), ((), ())), preferred_element_type=F32)

    def tn(a, b):
        return lax.dot_general(a, b, (((0,), (0,)), ((), ())), preferred_element_type=F32)

    def chunk(ci, carry):
        r0 = pl.multiple_of(ci * c, c)
        rows = pl.ds(r0, c)
        lf = lf_ref[rows, :]
        hi = lf.astype(BF16)
        r1 = lf - hi.astype(F32)
        mid = r1.astype(BF16)
        lo = (r1 - mid.astype(F32)).astype(BF16)
        cum = (jnp.dot(tri, hi.astype(mm_dtype), preferred_element_type=F32)
               + jnp.dot(tri, mid.astype(mm_dtype), preferred_element_type=F32)
               + jnp.dot(tri, lo.astype(mm_dtype), preferred_element_type=F32))
        qa = qa_ref[rows, :]
        kk = kk_ref[rows, :]
        vv = vv_ref[rows, :]
        cum_end = cum[c - 1:c, :]
        q_in = qa * jnp.exp(cum)
        k_out = kk * jnp.exp(cum_end - cum)
        dec_end = jnp.exp(cum_end)
        safe = jnp.min(cum) >= HGRN_SAFE_LOG_DECAY

        def fast():
            k_rel = kk * jnp.exp(-cum)
            return [nt(q_in[:, h * SEG:(h + 1) * SEG].astype(mm_dtype),
                       k_rel[:, h * SEG:(h + 1) * SEG].astype(mm_dtype)) for h in range(N_SEG)]

        def exact():
            lsc_ref[...] = cum
            ksc_ref[...] = kk

            def col(s, accs):
                w = jnp.exp(jnp.minimum(cum - lsc_ref[pl.ds(s, 1), :], 0.0))
                prod = qa * ksc_ref[pl.ds(s, 1), :] * w
                return [jnp.where(c_i == s,
                                  jnp.sum(prod[:, h * SEG:(h + 1) * SEG], axis=-1, keepdims=True), accs[h])
                        for h in range(N_SEG)]

            return lax.fori_loop(0, c, col, [jnp.zeros((c, c), F32)] * N_SEG)

        a_all = lax.cond(safe, fast, exact)

        for h in range(N_SEG):
            cols = slice(h * SEG, (h + 1) * SEG)
            st = st_ref[h]
            a_h = jnp.where(causal, a_all[h], 0.0)
            v_h = vv[:, cols].astype(mm_dtype)
            o = (nt(q_in[:, cols].astype(mm_dtype), st.astype(mm_dtype))
                 + jnp.dot(a_h.astype(mm_dtype), v_h, preferred_element_type=F32))
            st_ref[h] = st * dec_end[:, cols] + tn(v_h, k_out[:, cols].astype(mm_dtype))
            on = o * lax.rsqrt(jnp.mean(o * o, axis=-1, keepdims=True) + RMS_EPS) * nw_ref[:, cols]
            o_ref[rows, cols] = (on * sg_ref[rows, cols]).astype(BF16)
        return carry

    lax.fori_loop(0, n_chunks, chunk, 0)

    @pl.when(j == pl.num_programs(1) - 1)
    def _():
        for h in range(N_SEG):
            sfin_ref[0, h] = st_ref[h].T


def _hgrn(qa, kk, lf, vv, sg, norm_w, state0, n_seq, t_len, tc, c):
    n = qa.shape[0]
    steps = t_len // tc
    mm_dtype = BF16 if c % (2 * V7X_SUBLANES) == 0 else F32
    row = pl.BlockSpec((tc, D_BR), lambda i, j: (i * steps + j, 0))
    st_spec = pl.BlockSpec((1, N_SEG, SEG, SEG), lambda i, j: (i, 0, 0, 0))
    return pl.pallas_call(
        functools.partial(_hgrn_kernel, c=c, n_chunks=tc // c, mm_dtype=mm_dtype),
        grid=(n_seq, steps),
        in_specs=[row, row, row, row, row, _const_spec(norm_w.shape), st_spec],
        out_specs=[row, st_spec],
        out_shape=[jax.ShapeDtypeStruct((n, D_BR), BF16),
                   jax.ShapeDtypeStruct((n_seq, N_SEG, SEG, SEG), F32)],
        scratch_shapes=[pltpu.VMEM((N_SEG, SEG, SEG), F32), pltpu.VMEM((c, D_BR), F32),
                        pltpu.VMEM((c, D_BR), F32)],
        compiler_params=_cparams(("arbitrary", "arbitrary")),
        name="hgrn",
    )(qa, kk, lf, vv, sg, norm_w, state0)


def _attn_prompt_kernel(q_ref, kt_ref, v_ref, o_ref, m_ref, l_ref, acc_ref, *, tq, tk):
    qi = pl.program_id(1)
    q = q_ref[0]
    rows = tq * MLA_HEADS
    m_ref[...] = jnp.full((rows, 1), NEG_BIG, F32)
    l_ref[...] = jnp.zeros((rows, 1), F32)
    acc_ref[...] = jnp.zeros((rows, KV_RANK), F32)

    def step(ks, masked):
        k = kt_ref[0, :, pl.ds(ks, tk)]
        s = jnp.dot(q, k, preferred_element_type=F32)
        if masked:
            tok = qi * tq + lax.broadcasted_iota(jnp.int32, (rows, tk), 0) // MLA_HEADS
            key = ks + lax.broadcasted_iota(jnp.int32, (rows, tk), 1)
            s = jnp.where(key <= tok, s, NEG_BIG)
        m_old = m_ref[...]
        m_new = jnp.maximum(m_old, jnp.max(s, axis=-1, keepdims=True))
        alpha = jnp.exp(m_old - m_new)
        p = jnp.exp(s - m_new)
        l_ref[...] = alpha * l_ref[...] + jnp.sum(p, axis=-1, keepdims=True)
        acc_ref[...] = alpha * acc_ref[...] + jnp.dot(p.astype(BF16), v_ref[0, pl.ds(ks, tk), :],
                                                      preferred_element_type=F32)
        m_ref[...] = m_new

    n_full = (qi * tq) // tk

    def body(jb, carry):
        step(pl.multiple_of(jb * tk, tk), False)
        return carry

    lax.fori_loop(0, n_full, body, 0)
    step(pl.multiple_of(n_full * tk, tk), True)
    o_ref[0] = (acc_ref[...] / l_ref[...]).astype(BF16)


def _attn_prompt(q, kt, v, tq, tk):
    b, _, t = kt.shape
    assert t % tk == 0 and tk % tq == 0
    rows = tq * MLA_HEADS
    return pl.pallas_call(
        functools.partial(_attn_prompt_kernel, tq=tq, tk=tk),
        grid=(b, t // tq),
        in_specs=[pl.BlockSpec((1, rows, D_QK), lambda i, j: (i, j, 0)),
                  pl.BlockSpec((1, D_QK, t), lambda i, j: (i, 0, 0)),
                  pl.BlockSpec((1, t, KV_RANK), lambda i, j: (i, 0, 0))],
        out_specs=pl.BlockSpec((1, rows, KV_RANK), lambda i, j: (i, j, 0)),
        out_shape=jax.ShapeDtypeStruct((b, t * MLA_HEADS, KV_RANK), BF16),
        scratch_shapes=[pltpu.VMEM((rows, 1), F32), pltpu.VMEM((rows, 1), F32),
                        pltpu.VMEM((rows, KV_RANK), F32)],
        compiler_params=_cparams(("arbitrary", "arbitrary")),
        name="attn_prompt",
    )(q, kt, v)


def _attn_sample_kernel(layer_ref, pt_ref, q_ref, kn_ref, vn_ref, ckv_hbm, kpe_hbm, o_ref,
                        cbuf, pbuf, sem, *, n_pages, page, t_new):
    b = pl.program_id(0)
    nb = pl.num_programs(0)
    slot = b % 2
    layer = layer_ref[0]

    def page_copies(seq, slot_, p):
        pg = pt_ref[seq, p]
        dst = pl.ds(p * page, page)
        return (pltpu.make_async_copy(ckv_hbm.at[layer, pg], cbuf.at[slot_, dst, :], sem.at[0, slot_]),
                pltpu.make_async_copy(kpe_hbm.at[layer, pg], pbuf.at[slot_, dst, :], sem.at[1, slot_]))

    def start_all(seq, slot_):
        for p in range(n_pages):
            for cp in page_copies(seq, slot_, p):
                cp.start()

    @pl.when(b == 0)
    def _():
        start_all(0, 0)

    @pl.when(b + 1 < nb)
    def _():
        start_all(b + 1, 1 - slot)

    for p in range(n_pages):
        for cp in page_copies(b, slot, p):
            cp.wait()

    q = q_ref[0]
    kc = cbuf[slot].astype(BF16)
    kcat = jnp.concatenate([kc, pbuf[slot].astype(BF16)], axis=1)
    nt = (((1,), (1,)), ((), ()))
    s_past = lax.dot_general(q, kcat, nt, preferred_element_type=F32)
    s_new = lax.dot_general(q.astype(F32), kn_ref[0], nt, preferred_element_type=F32)
    rows = t_new * MLA_HEADS
    tok = lax.broadcasted_iota(jnp.int32, (rows, t_new), 0) // MLA_HEADS
    key = lax.broadcasted_iota(jnp.int32, (rows, t_new), 1)
    s_new = jnp.where(key <= tok, s_new, NEG_BIG)
    m = jnp.maximum(jnp.max(s_past, axis=-1, keepdims=True), jnp.max(s_new, axis=-1, keepdims=True))
    p_past = jnp.exp(s_past - m)
    p_new = jnp.exp(s_new - m)
    denom = jnp.sum(p_past, axis=-1, keepdims=True) + jnp.sum(p_new, axis=-1, keepdims=True)
    o = (jnp.dot(p_past.astype(BF16), kc, preferred_element_type=F32)
         + jnp.dot(p_new, vn_ref[0], preferred_element_type=F32))
    o_ref[0] = (o / denom).astype(BF16)


def _attn_sample(layer, page_table, q, k_new, v_new, cache_ckv, cache_kpe):
    n_seq, n_pages = page_table.shape
    page = cache_ckv.shape[2]
    t_new = k_new.shape[1]
    rows = t_new * MLA_HEADS
    past = n_pages * page
    grid_spec = pltpu.PrefetchScalarGridSpec(
        num_scalar_prefetch=2,
        grid=(n_seq,),
        in_specs=[pl.BlockSpec((1, rows, D_QK), lambda i, *_: (i, 0, 0)),
                  pl.BlockSpec((1, t_new, D_QK), lambda i, *_: (i, 0, 0)),
                  pl.BlockSpec((1, t_new, KV_RANK), lambda i, *_: (i, 0, 0)),
                  pl.BlockSpec(memory_space=pl.ANY), pl.BlockSpec(memory_space=pl.ANY)],
        out_specs=pl.BlockSpec((1, rows, KV_RANK), lambda i, *_: (i, 0, 0)),
        scratch_shapes=[pltpu.VMEM((2, past, KV_RANK), F32), pltpu.VMEM((2, past, MLA_ROPE), F32),
                        pltpu.SemaphoreType.DMA((2, 2))],
    )
    return pl.pallas_call(
        functools.partial(_attn_sample_kernel, n_pages=n_pages, page=page, t_new=t_new),
        grid_spec=grid_spec,
        out_shape=jax.ShapeDtypeStruct((n_seq, rows, KV_RANK), BF16),
        compiler_params=_cparams(("arbitrary",)),
        name="attn_sample",
    )(layer, page_table, q, k_new, v_new, cache_ckv, cache_kpe)


def _merge_kernel(x_ref, yp_ref, oh_ref, ol_ref, gate_ref, wuv_ref, wp_ref, wh_ref, wm_ref, wo_ref,
                  g_ref, b_ref, y_ref, *, alpha):
    dot = functools.partial(jnp.dot, preferred_element_type=F32)
    o_mla = dot(ol_ref[...], wuv_ref[...]).astype(BF16)
    merged = (gate_ref[:, :D_MODEL] * dot(yp_ref[...], wp_ref[...])
              + gate_ref[:, D_MODEL:2 * D_MODEL] * dot(oh_ref[...], wh_ref[...])
              + gate_ref[:, 2 * D_MODEL:] * dot(o_mla, wm_ref[...]))
    h = alpha * x_ref[...] + dot(merged.astype(BF16), wo_ref[...])
    y_ref[...] = _layer_norm(h, g_ref[...], b_ref[...])


def _merge(x, y_pool, o_hg, o_lat, gates, wuv, wp, wh, wm, wo, g, b, alpha, tm):
    n = x.shape[0]
    row = lambda w: pl.BlockSpec((tm, w), lambda i: (i, 0))
    consts = (wuv, wp, wh, wm, wo, g, b)
    return pl.pallas_call(
        functools.partial(_merge_kernel, alpha=alpha),
        grid=(n // tm,),
        in_specs=[row(D_MODEL), row(D_BR), row(D_BR), row(MLA_HEADS * KV_RANK), row(3 * D_MODEL)]
        + [_const_spec(a.shape) for a in consts],
        out_specs=row(D_MODEL),
        out_shape=jax.ShapeDtypeStruct((n, D_MODEL), F32),
        compiler_params=_cparams(("arbitrary",)),
        name="merge",
    )(x, y_pool, o_hg, o_lat, gates, *consts)


def _ffn_kernel(x_ref, wup_ref, cw_ref, cb_ref, wdn_ref, g_ref, b_ref, st_ref, y_ref, tail_ref, halo_ref,
                *, alpha, tm, seq_rows):
    x = x_ref[...]
    up = jnp.dot(x.astype(BF16), wup_ref[...], preferred_element_type=F32)
    r = lax.broadcasted_iota(jnp.int32, (tm, 1), 0)
    prev1 = pltpu.roll(up, 1, 0)
    prev2 = pltpu.roll(up, 2, 0)
    if seq_rows % tm == 0:
        i = pl.program_id(0)

        @pl.when(i % (seq_rows // tm) == 0)
        def _():
            halo_ref[...] = jnp.zeros(halo_ref.shape, F32)

        last = halo_ref[V7X_SUBLANES - 1:V7X_SUBLANES, :]
        last2 = halo_ref[V7X_SUBLANES - 2:V7X_SUBLANES - 1, :]
        prev1 = jnp.where(r == 0, last, prev1)
        prev2 = jnp.where(r == 0, last2, jnp.where(r == 1, last, prev2))
        halo_ref[...] = up[tm - V7X_SUBLANES:, :]
        tail_ref[0] = up[tm - V7X_SUBLANES:, :]
    else:
        n_s = tm // seq_rows
        t = r % seq_rows
        st = st_ref[...]
        wide = (n_s, seq_rows, 2 * D_FF)
        last = jnp.broadcast_to(st[:, 1:2, :], wide).reshape(tm, 2 * D_FF)
        last2 = jnp.broadcast_to(st[:, 0:1, :], wide).reshape(tm, 2 * D_FF)
        prev1 = jnp.where(t == 0, last, prev1)
        prev2 = jnp.where(t == 0, last2, jnp.where(t == 1, last, prev2))
        tail_ref[...] = up
    h = cb_ref[...] + prev2 * cw_ref[0:1, :] + prev1 * cw_ref[1:2, :] + up * cw_ref[2:3, :]
    gate = h[:, :D_FF]
    act = (gate * _sigmoid(gate) * h[:, D_FF:]).astype(BF16)
    f = jnp.dot(act, wdn_ref[...], preferred_element_type=F32)
    y_ref[...] = _layer_norm(alpha * x + f, g_ref[...], b_ref[...])


def _ffn(x, wup, cw, cb, wdn, g, b, state, alpha, tm, seq_rows):
    n = x.shape[0]
    row = pl.BlockSpec((tm, D_MODEL), lambda i: (i, 0))
    long_seq = seq_rows % tm == 0
    if long_seq:
        st_spec = _const_spec(state.shape)
        tail_spec = pl.BlockSpec((1, V7X_SUBLANES, 2 * D_FF), lambda i: (i, 0, 0))
        tail_shape = jax.ShapeDtypeStruct((n // tm, V7X_SUBLANES, 2 * D_FF), F32)
    else:
        n_s = tm // seq_rows
        st_spec = pl.BlockSpec((n_s, CONV_W - 1, 2 * D_FF), lambda i: (i, 0, 0))
        tail_spec = pl.BlockSpec((tm, 2 * D_FF), lambda i: (i, 0))
        tail_shape = jax.ShapeDtypeStruct((n, 2 * D_FF), F32)
    consts = (wup, cw, cb, wdn, g, b)
    return pl.pallas_call(
        functools.partial(_ffn_kernel, alpha=alpha, tm=tm, seq_rows=seq_rows),
        grid=(n // tm,),
        in_specs=[row] + [_const_spec(a.shape) for a in consts] + [st_spec],
        out_specs=[row, tail_spec],
        out_shape=[jax.ShapeDtypeStruct((n, D_MODEL), F32), tail_shape],
        scratch_shapes=[pltpu.VMEM((V7X_SUBLANES, 2 * D_FF), F32)],
        compiler_params=_cparams(("arbitrary",)),
        name="ffn",
    )(x, *consts, state)


def _pack_weights(w_in, w_uq, w_uk, w_uv):
    depth = w_in.shape[0]
    c_kpe = C_K1
    x1 = w_in[:, :, c_kpe:c_kpe + ROPE_HALF]
    x2 = w_in[:, :, c_kpe + ROPE_HALF:c_kpe + MLA_ROPE]
    pad = jnp.zeros((depth, D_MODEL, V7X_LANES - MLA_ROPE), w_in.dtype)
    wa = jnp.concatenate([w_in[:, :, :c_kpe], x1, x2, pad, x2, x1, pad, w_in[:, :, c_kpe + MLA_ROPE:]],
                         axis=-1).astype(BF16)
    uq = w_uq.reshape(depth, Q_RANK, MLA_HEADS, MLA_NOPE + MLA_ROPE)
    wq = jnp.concatenate([
        uq[..., :MLA_NOPE].reshape(depth, Q_RANK, -1),
        uq[..., MLA_NOPE:MLA_NOPE + ROPE_HALF].reshape(depth, Q_RANK, -1),
        uq[..., MLA_NOPE + ROPE_HALF:].reshape(depth, Q_RANK, -1)], axis=-1).astype(BF16)
    eye = jnp.eye(MLA_HEADS, dtype=w_uk.dtype)
    uk = jnp.transpose(w_uk, (0, 2, 3, 1))
    wuk = (eye[None, :, None, :, None] * uk[:, :, :, None, :]).reshape(
        depth, MLA_HEADS * MLA_NOPE, MLA_HEADS * KV_RANK).astype(BF16)
    uv = jnp.transpose(w_uv, (0, 2, 1, 3))
    wuv = (eye[None, :, None, :, None] * uv[:, :, :, None, :]).reshape(
        depth, MLA_HEADS * KV_RANK, MLA_HEADS * MLA_V).astype(BF16)
    return wa, wq, wuk, wuv


def _rope_tables(pos):
    inv = ROPE_THETA ** (-jnp.arange(ROPE_HALF, dtype=F32) / ROPE_HALF)
    ang = pos.astype(F32)[:, None] * inv[None, :]
    reps = V7X_LANES // ROPE_HALF
    return jnp.tile(jnp.cos(ang), (1, reps)), jnp.tile(jnp.sin(ang), (1, reps))


def _heads_q(q_lat, q_pe):
    n = q_lat.shape[0]
    pe = q_pe.reshape(n, 2, MLA_HEADS, ROPE_HALF).transpose(0, 2, 1, 3).reshape(n, MLA_HEADS, MLA_ROPE)
    return jnp.concatenate([q_lat.reshape(n, MLA_HEADS, KV_RANK), pe], axis=-1).reshape(n * MLA_HEADS, D_QK)


def _tiles(n_rows):
    return dict(in_proj=min(256, n_rows), merge=min(512, n_rows), ffn=min(256, n_rows))


POOL_TILE = 512
HGRN_TILE = 256
HGRN_CHUNK = 32
ATTN_TQ = 256
ATTN_TK = 512


def _trunk(x_prompt, x_sample, cache_ckv, cache_kpe, state_pool, state_hgrn, state_conv, page_table,
           w_in, pool_w, pool_scale, hg_lower_bounds, hg_norm, q_norm, w_uq, kv_norm, w_uk, w_uv,
           w_br_pool, w_br_hg, w_br_mla, w_out, ln1_g, ln1_b, ffn_w_up, ffn_conv_w, ffn_conv_b,
           ffn_w_down, ln2_g, ln2_b):
    depth = w_in.shape[0]
    b, t, _ = x_prompt.shape
    sb, ts, _ = x_sample.shape
    past_len = page_table.shape[1] * cache_ckv.shape[2]
    alpha = (2 * depth) ** 0.25

    wa, wq, wuk, wuv = _pack_weights(w_in, w_uq, w_uk, w_uv)
    bf = lambda a: a.astype(BF16)
    pool_wb, wp, wh, wm, wo, wup, wdn = map(bf, (pool_w, w_br_pool, w_br_hg, w_br_mla, w_out, ffn_w_up,
                                                  ffn_w_down))
    row2 = lambda a: a.reshape(depth, 1, -1)
    pool_scale, hg_norm, q_norm, kv_norm, ln1_g, ln1_b, ln2_g, ln2_b, ffn_conv_b = map(
        row2, (pool_scale, hg_norm, q_norm, kv_norm, ln1_g, ln1_b, ln2_g, ln2_b, ffn_conv_b))

    np_, ns_ = b * t, sb * ts
    tp, tsm = _tiles(np_), _tiles(ns_)
    cos_p, sin_p = _rope_tables(jnp.arange(t, dtype=jnp.int32))
    cos_s, sin_s = _rope_tables(past_len + jnp.arange(tsm["in_proj"], dtype=jnp.int32) % ts)

    xp = x_prompt.reshape(np_, D_MODEL)
    xs = x_sample.reshape(ns_, D_MODEL)
    zero_state = jnp.zeros((b, N_SEG, SEG, SEG), F32)
    no_conv_state = jnp.zeros((1, CONV_W - 1, 2 * D_FF), F32)
    outs_p = [[] for _ in range(5)]
    outs_s = [[] for _ in range(5)]

    for l in range(depth):
        layer = jnp.full((1,), l, jnp.int32)

        (u, qa, kk, lf, vv, sg, q_lat, q_pe, ckv, kpe, gates) = _in_proj(
            layer, xp, wa[l], hg_lower_bounds, q_norm[l], kv_norm[l], wq[l], wuk[l], cos_p, sin_p,
            tp["in_proj"])
        y_pool = _pool_prompt(u.reshape(b, t, D_BR), pool_wb[l], pool_scale[l], min(POOL_TILE, t))
        o_hg, s_fin = _hgrn(qa, kk, lf, vv, sg, hg_norm[l], zero_state, b, t, min(HGRN_TILE, t),
                            min(HGRN_CHUNK, t))
        kcat = jnp.concatenate([ckv, kpe], axis=-1).astype(BF16).reshape(b, t, D_QK)
        o_lat = _attn_prompt(_heads_q(q_lat, q_pe).reshape(b, t * MLA_HEADS, D_QK),
                             jnp.swapaxes(kcat, 1, 2), bf(ckv).reshape(b, t, KV_RANK),
                             min(ATTN_TQ, t), min(ATTN_TK, t))
        x1 = _merge(xp, y_pool.reshape(np_, D_BR), o_hg, o_lat.reshape(np_, MLA_HEADS * KV_RANK), gates,
                    wuv[l], wp[l], wh[l], wm[l], wo[l], ln1_g[l], ln1_b[l], alpha, tp["merge"])
        xp, tail = _ffn(x1, wup[l], ffn_conv_w[l], ffn_conv_b[l], wdn[l], ln2_g[l], ln2_b[l],
                        no_conv_state, alpha, tp["ffn"], t)
        tiles_per_seq = t // tp["ffn"]
        conv_p = tail.reshape(b, tiles_per_seq, V7X_SUBLANES, 2 * D_FF)[:, -1, -(CONV_W - 1):, :]
        for lst, v in zip(outs_p, (ckv.reshape(b, t, KV_RANK), kpe.reshape(b, t, MLA_ROPE),
                                   u.reshape(b, t, D_BR)[:, -POOL_BUF:], s_fin, conv_p)):
            lst.append(v)

        (u, qa, kk, lf, vv, sg, q_lat, q_pe, ckv, kpe, gates) = _in_proj(
            layer, xs, wa[l], hg_lower_bounds, q_norm[l], kv_norm[l], wq[l], wuk[l], cos_s, sin_s,
            tsm["in_proj"])
        ext = jnp.concatenate([state_pool[l], u.reshape(sb, ts, D_BR)], axis=1)
        y_pool = _pool_sample(jnp.swapaxes(ext, 0, 1), pool_wb[l], pool_scale[l], ts, past_len)
        y_pool = jnp.swapaxes(y_pool, 0, 1).reshape(ns_, D_BR)
        o_hg, s_fin = _hgrn(qa, kk, lf, vv, sg, hg_norm[l], state_hgrn[l], sb, ts, ts, ts)
        k_new = jnp.concatenate([ckv, kpe], axis=-1).reshape(sb, ts, D_QK)
        o_lat = _attn_sample(layer, page_table, _heads_q(q_lat, q_pe).reshape(sb, ts * MLA_HEADS, D_QK),
                             k_new, ckv.reshape(sb, ts, KV_RANK), cache_ckv, cache_kpe)
        x1 = _merge(xs, y_pool, o_hg, o_lat.reshape(ns_, MLA_HEADS * KV_RANK), gates,
                    wuv[l], wp[l], wh[l], wm[l], wo[l], ln1_g[l], ln1_b[l], alpha, tsm["merge"])
        xs, up = _ffn(x1, wup[l], ffn_conv_w[l], ffn_conv_b[l], wdn[l], ln2_g[l], ln2_b[l],
                      state_conv[l], alpha, tsm["ffn"], ts)
        conv_s = jnp.concatenate([state_conv[l], up.reshape(sb, ts, 2 * D_FF)], axis=1)[:, -(CONV_W - 1):]
        for lst, v in zip(outs_s, (ckv.reshape(sb, ts, KV_RANK), kpe.reshape(sb, ts, MLA_ROPE),
                                   ext[:, -POOL_BUF:], s_fin, conv_s)):
            lst.append(v)

    stack = lambda lists: [jnp.stack(v) for v in lists]
    return (xp.reshape(b, t, D_MODEL), xs.reshape(sb, ts, D_MODEL), *stack(outs_p), *stack(outs_s))


def kernel(x_prompt, x_sample, cache_ckv, cache_kpe, state_pool, state_hgrn, state_conv, page_table, w_in, pool_w, pool_scale, hg_lower_bounds, hg_norm, q_norm, w_uq, kv_norm, w_uk, w_uv, w_br_pool, w_br_hg, w_br_mla, w_out, ln1_g, ln1_b, ffn_w_up, ffn_conv_w, ffn_conv_b, ffn_w_down, ln2_g, ln2_b):
    return _trunk(x_prompt, x_sample, cache_ckv, cache_kpe, state_pool, state_hgrn, state_conv, page_table,
                  w_in, pool_w, pool_scale, hg_lower_bounds, hg_norm, q_norm, w_uq, kv_norm, w_uk, w_uv,
                  w_br_pool, w_br_hg, w_br_mla, w_out, ln1_g, ln1_b, ffn_w_up, ffn_conv_w, ffn_conv_b,
                  ffn_w_down, ln2_g, ln2_b)
```

```python
import functools

import jax
import jax.numpy as jnp
from jax import lax
from jax.experimental import pallas as pl
from jax.experimental.pallas import tpu as pltpu

F32 = jnp.float32
BF16 = jnp.bfloat16

D_MODEL = 1024
POOL_WINDOWS = (2, 4, 8, 16)
SEG = 128
N_SEG = 4
D_BR = N_SEG * SEG
POOL_BUF = max(POOL_WINDOWS) - 1
MLA_HEADS = 8
MLA_NOPE = 64
MLA_ROPE = 32
ROPE_HALF = MLA_ROPE // 2
MLA_V = 64
Q_RANK = 256
KV_RANK = SEG
D_QK = KV_RANK + MLA_ROPE
D_FF = 2816
CONV_W = 3
MLA_SCALE = (MLA_NOPE + MLA_ROPE) ** -0.5
ROPE_THETA = 10000.0
LN_EPS = 1e-5
RMS_EPS = 1e-6

V7X_VMEM_BYTES = 64 * 1024 * 1024
V7X_LANES = 128
V7X_SUBLANES = 8
VMEM_LIMIT = V7X_VMEM_BYTES - 6 * 1024 * 1024

C_POOL, C_HQ, C_HF, C_HI, C_HG, C_CQ, C_CKV, C_K1, C_K2, C_GATE, C_END = (
    0, 512, 1024, 1536, 2048, 2560, 2816, 2944, 3072, 3200, 3200 + 3 * D_MODEL)

NEG_BIG = -1e30
HGRN_SAFE_LOG_DECAY = -60.0


def _cparams(sem):
    return pltpu.CompilerParams(dimension_semantics=sem, vmem_limit_bytes=VMEM_LIMIT)


def _const_spec(shape):
    nd = len(shape)
    return pl.BlockSpec(shape, lambda *_: (0,) * nd, pipeline_mode=pl.Buffered(1))


def _sigmoid(x):
    return 1.0 / (1.0 + jnp.exp(-x))


def _layer_norm(h, g, b):
    mu = jnp.mean(h, axis=-1, keepdims=True)
    d = h - mu
    var = jnp.mean(d * d, axis=-1, keepdims=True)
    return d * lax.rsqrt(var + LN_EPS) * g + b


def _in_proj_kernel(layer_ref, x_ref, w_ref, hglb_ref, qn_ref, kvn_ref, wq_ref, wuk_ref, cos_ref, sin_ref,
                    u_ref, qa_ref, kk_ref, lf_ref, vv_ref, sg_ref, qlat_ref, qpe_ref, ckv_ref, kpe_ref,
                    kcat_ref, gate_ref):
    xb = x_ref[...].astype(BF16)

    def seg(lo, hi):
        return jnp.dot(xb, w_ref[:, lo:hi], preferred_element_type=F32)

    u_ref[...] = seg(C_POOL, C_HQ)

    hq = seg(C_HQ, C_HF)
    qa_ref[...] = hq * _sigmoid(hq)

    layer = layer_ref[0]
    hb = hglb_ref[...]
    e = jnp.exp(hb - jnp.max(hb, axis=0, keepdims=True))
    row = lax.broadcasted_iota(jnp.int32, hb.shape, 0)
    lb = (jnp.sum(jnp.where((row >= 1) & (row <= layer), e, 0.0), axis=0, keepdims=True)
          / jnp.sum(e, axis=0, keepdims=True))

    fl = seg(C_HF, C_HI)
    log_sig = jnp.minimum(fl, 0.0) - jnp.log1p(jnp.exp(-jnp.abs(fl)))
    a = jnp.log(lb)
    b = jnp.log1p(-lb) + log_sig
    lf_ref[...] = jnp.maximum(a, b) + jnp.log1p(jnp.exp(-jnp.abs(a - b)))
    kk_ref[...] = (1.0 - lb) * _sigmoid(-fl)

    vv_ref[...] = seg(C_HI, C_HG)
    hg = seg(C_HG, C_CQ)
    sg_ref[...] = hg * _sigmoid(hg)

    cq = seg(C_CQ, C_CKV)
    cqn = cq * lax.rsqrt(jnp.mean(cq * cq, axis=-1, keepdims=True) + RMS_EPS) * qn_ref[...]
    q = jnp.dot(cqn.astype(BF16), wq_ref[...], preferred_element_type=F32)
    cos = cos_ref[...]
    sin = sin_ref[...]
    n_nope = MLA_HEADS * MLA_NOPE
    x1 = q[:, n_nope:n_nope + V7X_LANES]
    x2 = q[:, n_nope + V7X_LANES:]
    qpe_ref[:, :V7X_LANES] = ((x1 * cos - x2 * sin) * MLA_SCALE).astype(BF16)
    qpe_ref[:, V7X_LANES:] = ((x1 * sin + x2 * cos) * MLA_SCALE).astype(BF16)
    qlat = jnp.dot(q[:, :n_nope].astype(BF16), wuk_ref[...], preferred_element_type=F32)
    qlat_ref[...] = (qlat * MLA_SCALE).astype(BF16)

    ckv = seg(C_CKV, C_K1)
    ckv = ckv * lax.rsqrt(jnp.mean(ckv * ckv, axis=-1, keepdims=True) + RMS_EPS) * kvn_ref[...]
    ckv_ref[...] = ckv
    kcat_ref[:, :KV_RANK] = ckv.astype(BF16)

    k1 = seg(C_K1, C_K2)
    k2 = seg(C_K2, C_GATE)
    lane = lax.broadcasted_iota(jnp.int32, sin.shape, 1)
    kr = k1 * cos + k2 * jnp.where(lane < ROPE_HALF, -sin, sin)
    kpe_ref[...] = kr[:, :MLA_ROPE]
    kcat_ref[:, KV_RANK:] = kr[:, :MLA_ROPE].astype(BF16)

    gate_ref[...] = _sigmoid(seg(C_GATE, C_END))


def _in_proj(layer, x, wa, hglb, q_norm, kv_norm, wq, wuk, cos_tab, sin_tab, tm):
    n = x.shape[0]
    tab_blocks = cos_tab.shape[0] // tm
    row = lambda w: pl.BlockSpec((tm, w), lambda i, *_: (i, 0))
    tab = pl.BlockSpec((tm, V7X_LANES), lambda i, *_: (i % tab_blocks, 0))
    widths = (D_BR,) * 6 + (MLA_HEADS * KV_RANK, MLA_HEADS * MLA_ROPE, KV_RANK, MLA_ROPE, D_QK, 3 * D_MODEL)
    dtypes = (F32,) * 6 + (BF16, BF16, F32, F32, BF16, F32)
    grid_spec = pltpu.PrefetchScalarGridSpec(
        num_scalar_prefetch=1,
        grid=(n // tm,),
        in_specs=[row(D_MODEL), _const_spec(wa.shape), _const_spec(hglb.shape), _const_spec(q_norm.shape),
                  _const_spec(kv_norm.shape), _const_spec(wq.shape), _const_spec(wuk.shape), tab, tab],
        out_specs=[row(w) for w in widths],
    )
    return pl.pallas_call(
        _in_proj_kernel,
        grid_spec=grid_spec,
        out_shape=[jax.ShapeDtypeStruct((n, w), d) for w, d in zip(widths, dtypes)],
        compiler_params=_cparams(("arbitrary",)),
        name="in_proj",
    )(layer, x, wa, hglb, q_norm, kv_norm, wq, wuk, cos_tab, sin_tab)


def _pool_prompt_kernel(u_ref, w_ref, sc_ref, y_ref, ext_ref, *, tt):
    halo = 2 * V7X_SUBLANES
    j = pl.program_id(1)

    @pl.when(j == 0)
    def _():
        ext_ref[0:halo, :] = jnp.zeros((halo, D_BR), F32)

    u = u_ref[0]
    ext_ref[halo:halo + tt, :] = u
    pos = j * tt + lax.broadcasted_iota(jnp.int32, (tt, 1), 0)
    for g, win in enumerate(POOL_WINDOWS):
        c0 = g * SEG
        acc = u[:, c0:c0 + SEG]
        for d in range(1, win):
            acc = acc + ext_ref[halo - d:halo - d + tt, c0:c0 + SEG]
        cnt = jnp.minimum(pos + 1, win).astype(F32)
        diff = (acc / cnt - u[:, c0:c0 + SEG]).astype(BF16)
        y = jnp.dot(diff, w_ref[g], preferred_element_type=F32) * sc_ref[:, c0:c0 + SEG]
        y_ref[0, :, c0:c0 + SEG] = y.astype(BF16)
    ext_ref[0:halo, :] = ext_ref[tt:tt + halo, :]


def _pool_prompt(u, w_pool, scale, tt):
    b, t, _ = u.shape
    return pl.pallas_call(
        functools.partial(_pool_prompt_kernel, tt=tt),
        grid=(b, t // tt),
        in_specs=[pl.BlockSpec((1, tt, D_BR), lambda i, j: (i, j, 0)), _const_spec(w_pool.shape),
                  _const_spec(scale.shape)],
        out_specs=pl.BlockSpec((1, tt, D_BR), lambda i, j: (i, j, 0)),
        out_shape=jax.ShapeDtypeStruct((b, t, D_BR), BF16),
        scratch_shapes=[pltpu.VMEM((tt + 2 * V7X_SUBLANES, D_BR), F32)],
        compiler_params=_cparams(("arbitrary", "arbitrary")),
        name="pool_prompt",
    )(u, w_pool, scale)


def _pool_sample_kernel(ext_ref, w_ref, sc_ref, y_ref, *, t_new, past_len):
    s = ext_ref.shape[1]
    for g, win in enumerate(POOL_WINDOWS):
        c0 = g * SEG
        diffs = []
        for t in range(t_new):
            cur = ext_ref[POOL_BUF + t, :, c0:c0 + SEG]
            acc = cur
            for d in range(1, win):
                acc = acc + ext_ref[POOL_BUF + t - d, :, c0:c0 + SEG]
            cnt = float(min(past_len + t + 1, win))
            diffs.append(acc / cnt - cur)
        diff = jnp.concatenate(diffs, axis=0).astype(BF16)
        y = jnp.dot(diff, w_ref[g], preferred_element_type=F32) * sc_ref[:, c0:c0 + SEG]
        y_ref[:, :, c0:c0 + SEG] = y.reshape(t_new, s, SEG).astype(BF16)


def _pool_sample(ext_tm, w_pool, scale, t_new, past_len):
    _, s, _ = ext_tm.shape
    return pl.pallas_call(
        functools.partial(_pool_sample_kernel, t_new=t_new, past_len=past_len),
        grid=(1,),
        in_specs=[_const_spec(ext_tm.shape), _const_spec(w_pool.shape), _const_spec(scale.shape)],
        out_specs=pl.BlockSpec((t_new, s, D_BR), lambda i: (0, 0, 0)),
        out_shape=jax.ShapeDtypeStruct((t_new, s, D_BR), BF16),
        compiler_params=_cparams(("arbitrary",)),
        name="pool_sample",
    )(ext_tm, w_pool, scale)


def _hgrn_kernel(qa_ref, kk_ref, lf_ref, vv_ref, sg_ref, nw_ref, s0_ref, o_ref, sfin_ref,
                 st_ref, lsc_ref, ksc_ref, *, c, n_chunks, chained):
    tile = c * n_chunks
    log2c = c.bit_length() - 1
    j = pl.program_id(1)

    if chained:
        @pl.when(j == 0)
        def _():
            for h in range(N_SEG):
                st_ref[h] = s0_ref[0, h].T

    r_i = lax.broadcasted_iota(jnp.int32, (tile, tile), 0)
    c_i = lax.broadcasted_iota(jnp.int32, (tile, tile), 1)
    chunk_of_row = jnp.right_shift(r_i, log2c)
    blk_causal = (chunk_of_row == jnp.right_shift(c_i, log2c)) & (r_i >= c_i)
    tri = blk_causal.astype(BF16)

    chunk_of_col = jnp.right_shift(lax.broadcasted_iota(jnp.int32, (SEG, tile), 1), log2c)

    def nt(a, b):
        return lax.dot_general(a, b, (((1,), (1,)), ((), ())), preferred_element_type=F32)

    def per_chunk(x3):
        return jnp.broadcast_to(x3, (n_chunks, c, D_BR)).reshape(tile, D_BR)

    lf = lf_ref[...]
    hi = lf.astype(BF16)
    r1 = lf - hi.astype(F32)
    mid = r1.astype(BF16)
    lo = (r1 - mid.astype(F32)).astype(BF16)
    cum = (jnp.dot(tri, hi, preferred_element_type=F32) + jnp.dot(tri, mid, preferred_element_type=F32)
           + jnp.dot(tri, lo, preferred_element_type=F32))
    cum3 = cum.reshape(n_chunks, c, D_BR)
    tot = per_chunk(cum3[:, c - 1:c, :])
    qa = qa_ref[...]
    kk = kk_ref[...]
    vv = vv_ref[...]
    q_in = qa * jnp.exp(cum)
    k_out = kk * jnp.exp(tot - cum)
    dec = jnp.exp(tot)
    safe = jnp.min(cum) >= HGRN_SAFE_LOG_DECAY

    def fast():
        k_rel = kk * jnp.exp(-cum)
        return [nt(q_in[:, h * SEG:(h + 1) * SEG].astype(BF16), k_rel[:, h * SEG:(h + 1) * SEG].astype(BF16))
                for h in range(N_SEG)]

    def exact():
        lsc_ref[...] = cum3
        ksc_ref[...] = kk.reshape(n_chunks, c, D_BR)

        def col(s, accs):
            w = jnp.exp(jnp.minimum(cum - per_chunk(lsc_ref[:, pl.ds(s, 1), :]), 0.0))
            prod = qa * per_chunk(ksc_ref[:, pl.ds(s, 1), :]) * w
            here = c_i == chunk_of_row * c + s
            return [jnp.where(here, jnp.sum(prod[:, h * SEG:(h + 1) * SEG], axis=-1, keepdims=True), accs[h])
                    for h in range(N_SEG)]

        return lax.fori_loop(0, c, col, [jnp.zeros((tile, tile), F32)] * N_SEG)

    a_all = lax.cond(safe, fast, exact)

    for h in range(N_SEG):
        cols = slice(h * SEG, (h + 1) * SEG)
        a_h = jnp.where(blk_causal, a_all[h], 0.0).astype(BF16)
        o_intra = jnp.dot(a_h, vv[:, cols].astype(BF16), preferred_element_type=F32)
        vt = vv[:, cols].T
        vexp_t = jnp.concatenate([jnp.where(chunk_of_col == ci, vt, 0.0) for ci in range(n_chunks)],
                                 axis=0).astype(BF16)
        kv_all = jnp.dot(vexp_t, k_out[:, cols].astype(BF16), preferred_element_type=F32)
        states = []
        st = st_ref[h] if chained else None
        for ci in range(n_chunks):
            if not chained:
                st = s0_ref[ci, h].T
            states.append(st)
            st = st * dec[ci * c:ci * c + 1, cols] + kv_all[ci * SEG:(ci + 1) * SEG]
            if not chained:
                sfin_ref[ci, h] = st.T
        if chained:
            st_ref[h] = st
        p_all = nt(q_in[:, cols].astype(BF16), jnp.concatenate(states, axis=0).astype(BF16))
        o_inter = jnp.concatenate([p_all[ci * c:(ci + 1) * c, ci * SEG:(ci + 1) * SEG]
                                   for ci in range(n_chunks)], axis=0)
        o = o_intra + o_inter
        on = o * lax.rsqrt(jnp.mean(o * o, axis=-1, keepdims=True) + RMS_EPS) * nw_ref[:, cols]
        o_ref[:, cols] = (on * sg_ref[:, cols]).astype(BF16)

    if chained:
        @pl.when(j == pl.num_programs(1) - 1)
        def _():
            for h in range(N_SEG):
                sfin_ref[0, h] = st_ref[h].T


def _hgrn(qa, kk, lf, vv, sg, norm_w, state0, n_seq, t_len, tile, c):
    n = qa.shape[0]
    assert c & (c - 1) == 0 and tile % c == 0
    n_chunks = tile // c
    chained = t_len >= tile
    if chained:
        steps = t_len // tile
        grid = (n_seq, steps)
        row = pl.BlockSpec((tile, D_BR), lambda i, j: (i * steps + j, 0))
        st_spec = pl.BlockSpec((1, N_SEG, SEG, SEG), lambda i, j: (i, 0, 0, 0))
    else:
        assert t_len == c and n_seq % n_chunks == 0
        grid = (n_seq // n_chunks, 1)
        row = pl.BlockSpec((tile, D_BR), lambda i, j: (i, 0))
        st_spec = pl.BlockSpec((n_chunks, N_SEG, SEG, SEG), lambda i, j: (i, 0, 0, 0))
    return pl.pallas_call(
        functools.partial(_hgrn_kernel, c=c, n_chunks=n_chunks, chained=chained),
        grid=grid,
        in_specs=[row, row, row, row, row, _const_spec(norm_w.shape), st_spec],
        out_specs=[row, st_spec],
        out_shape=[jax.ShapeDtypeStruct((n, D_BR), BF16),
                   jax.ShapeDtypeStruct((n_seq, N_SEG, SEG, SEG), F32)],
        scratch_shapes=[pltpu.VMEM((N_SEG, SEG, SEG), F32), pltpu.VMEM((n_chunks, c, D_BR), F32),
                        pltpu.VMEM((n_chunks, c, D_BR), F32)],
        compiler_params=_cparams(("arbitrary", "arbitrary")),
        name="hgrn",
    )(qa, kk, lf, vv, sg, norm_w, state0)


def _attn_prompt_kernel(ql_ref, qp_ref, k_ref, vt_ref, o_ref, qt_ref, m_ref, l_ref, acc_ref, *, tq, tk):
    qi = pl.program_id(1)
    cols = tq * MLA_HEADS

    qpt = qp_ref[...].astype(F32).T
    for h in range(MLA_HEADS):
        cs = slice(h * tq, (h + 1) * tq)
        qt_ref[0:KV_RANK, cs] = ql_ref[:, h * KV_RANK:(h + 1) * KV_RANK].astype(F32).T.astype(BF16)
        qt_ref[KV_RANK:KV_RANK + ROPE_HALF, cs] = qpt[h * ROPE_HALF:(h + 1) * ROPE_HALF].astype(BF16)
        qt_ref[KV_RANK + ROPE_HALF:D_QK, cs] = qpt[V7X_LANES + h * ROPE_HALF:
                                                  V7X_LANES + (h + 1) * ROPE_HALF].astype(BF16)

    m_ref[...] = jnp.full((1, cols), NEG_BIG, F32)
    l_ref[...] = jnp.zeros((1, cols), F32)
    acc_ref[...] = jnp.zeros((KV_RANK, cols), F32)

    def step(ks, masked):
        s = jnp.dot(k_ref[0, pl.ds(ks, tk), :], qt_ref[...], preferred_element_type=F32)
        if masked:
            key = ks + lax.broadcasted_iota(jnp.int32, (tk, cols), 0)
            tok = qi * tq + (lax.broadcasted_iota(jnp.int32, (tk, cols), 1) & (tq - 1))
            s = jnp.where(key <= tok, s, NEG_BIG)
        m_old = m_ref[...]
        m_new = jnp.maximum(m_old, jnp.max(s, axis=0, keepdims=True))
        alpha = jnp.exp(m_old - m_new)
        p = jnp.exp(s - m_new)
        l_ref[...] = alpha * l_ref[...] + jnp.sum(p, axis=0, keepdims=True)
        acc_ref[...] = alpha * acc_ref[...] + jnp.dot(vt_ref[0, :, pl.ds(ks, tk)], p.astype(BF16),
                                                      preferred_element_type=F32)
        m_ref[...] = m_new

    n_full = (qi * tq) // tk

    def body(jb, carry):
        step(pl.multiple_of(jb * tk, tk), False)
        return carry

    lax.fori_loop(0, n_full, body, 0)
    step(pl.multiple_of(n_full * tk, tk), True)
    o = acc_ref[...] / l_ref[...]
    for h in range(MLA_HEADS):
        o_ref[:, h * KV_RANK:(h + 1) * KV_RANK] = o[:, h * tq:(h + 1) * tq].T.astype(BF16)


def _attn_prompt(q_lat, q_pe, k, vt, tq, tk):
    b, t, _ = k.shape
    assert t % tk == 0 and tk % tq == 0 and tq & (tq - 1) == 0
    nq = t // tq
    cols = tq * MLA_HEADS
    qrow = lambda w: pl.BlockSpec((tq, w), lambda i, j: (i * nq + j, 0))
    return pl.pallas_call(
        functools.partial(_attn_prompt_kernel, tq=tq, tk=tk),
        grid=(b, nq),
        in_specs=[qrow(MLA_HEADS * KV_RANK), qrow(MLA_HEADS * MLA_ROPE),
                  pl.BlockSpec((1, t, D_QK), lambda i, j: (i, 0, 0)),
                  pl.BlockSpec((1, KV_RANK, t), lambda i, j: (i, 0, 0))],
        out_specs=qrow(MLA_HEADS * KV_RANK),
        out_shape=jax.ShapeDtypeStruct((b * t, MLA_HEADS * KV_RANK), BF16),
        scratch_shapes=[pltpu.VMEM((D_QK, cols), BF16), pltpu.VMEM((1, cols), F32),
                        pltpu.VMEM((1, cols), F32), pltpu.VMEM((KV_RANK, cols), F32)],
        compiler_params=_cparams(("arbitrary", "arbitrary")),
        name="attn_prompt",
    )(q_lat, q_pe, k, vt)


def _attn_sample_kernel(layer_ref, pt_ref, q_ref, kn_ref, vn_ref, ckv_hbm, kpe_hbm, o_ref,
                        cbuf, pbuf, sem, *, n_pages, page, t_new):
    b = pl.program_id(0)
    nb = pl.num_programs(0)
    slot = b % 2
    layer = layer_ref[0]

    def page_copies(seq, slot_, p):
        pg = pt_ref[seq, p]
        dst = pl.ds(p * page, page)
        return (pltpu.make_async_copy(ckv_hbm.at[layer, pg], cbuf.at[slot_, dst, :], sem.at[0, slot_]),
                pltpu.make_async_copy(kpe_hbm.at[layer, pg], pbuf.at[slot_, :, dst], sem.at[1, slot_]))

    def start_all(seq, slot_):
        for p in range(n_pages):
            for cp in page_copies(seq, slot_, p):
                cp.start()

    @pl.when(b == 0)
    def _():
        start_all(0, 0)

    @pl.when(b + 1 < nb)
    def _():
        start_all(b + 1, 1 - slot)

    for p in range(n_pages):
        for cp in page_copies(b, slot, p):
            cp.wait()

    q = q_ref[0]
    kc = cbuf[slot].astype(BF16)
    nt = (((1,), (1,)), ((), ()))
    s_past = (lax.dot_general(q[:, :KV_RANK], kc, nt, preferred_element_type=F32)
              + jnp.dot(q[:, KV_RANK:], pbuf[slot].astype(BF16), preferred_element_type=F32))
    s_new = lax.dot_general(q.astype(F32), kn_ref[0], nt, preferred_element_type=F32)
    rows = t_new * MLA_HEADS
    tok = lax.broadcasted_iota(jnp.int32, (rows, t_new), 0) // MLA_HEADS
    key = lax.broadcasted_iota(jnp.int32, (rows, t_new), 1)
    s_new = jnp.where(key <= tok, s_new, NEG_BIG)
    m = jnp.maximum(jnp.max(s_past, axis=-1, keepdims=True), jnp.max(s_new, axis=-1, keepdims=True))
    p_past = jnp.exp(s_past - m)
    p_new = jnp.exp(s_new - m)
    denom = jnp.sum(p_past, axis=-1, keepdims=True) + jnp.sum(p_new, axis=-1, keepdims=True)
    o = (jnp.dot(p_past.astype(BF16), kc, preferred_element_type=F32)
         + jnp.dot(p_new, vn_ref[0], preferred_element_type=F32))
    o_ref[0] = (o / denom).astype(BF16)


def _attn_sample(layer, page_table, q, k_new, v_new, cache_ckv, cache_kpe_t):
    n_seq, n_pages = page_table.shape
    page = cache_ckv.shape[2]
    t_new = k_new.shape[1]
    rows = t_new * MLA_HEADS
    past = n_pages * page
    grid_spec = pltpu.PrefetchScalarGridSpec(
        num_scalar_prefetch=2,
        grid=(n_seq,),
        in_specs=[pl.BlockSpec((1, rows, D_QK), lambda i, *_: (i, 0, 0)),
                  pl.BlockSpec((1, t_new, D_QK), lambda i, *_: (i, 0, 0)),
                  pl.BlockSpec((1, t_new, KV_RANK), lambda i, *_: (i, 0, 0)),
                  pl.BlockSpec(memory_space=pl.ANY), pl.BlockSpec(memory_space=pl.ANY)],
        out_specs=pl.BlockSpec((1, rows, KV_RANK), lambda i, *_: (i, 0, 0)),
        scratch_shapes=[pltpu.VMEM((2, past, KV_RANK), F32), pltpu.VMEM((2, MLA_ROPE, past), F32),
                        pltpu.SemaphoreType.DMA((2, 2))],
    )
    return pl.pallas_call(
        functools.partial(_attn_sample_kernel, n_pages=n_pages, page=page, t_new=t_new),
        grid_spec=grid_spec,
        out_shape=jax.ShapeDtypeStruct((n_seq, rows, KV_RANK), BF16),
        compiler_params=_cparams(("arbitrary",)),
        name="attn_sample",
    )(layer, page_table, q, k_new, v_new, cache_ckv, cache_kpe_t)


def _merge_kernel(x_ref, yp_ref, oh_ref, ol_ref, gate_ref, wuv_ref, wp_ref, wh_ref, wm_ref, wo_ref,
                  g_ref, b_ref, y_ref, *, alpha):
    dot = functools.partial(jnp.dot, preferred_element_type=F32)
    o_mla = dot(ol_ref[...], wuv_ref[...]).astype(BF16)
    merged = (gate_ref[:, :D_MODEL] * dot(yp_ref[...], wp_ref[...])
              + gate_ref[:, D_MODEL:2 * D_MODEL] * dot(oh_ref[...], wh_ref[...])
              + gate_ref[:, 2 * D_MODEL:] * dot(o_mla, wm_ref[...]))
    h = alpha * x_ref[...] + dot(merged.astype(BF16), wo_ref[...])
    y_ref[...] = _layer_norm(h, g_ref[...], b_ref[...])


def _merge(x, y_pool, o_hg, o_lat, gates, wuv, wp, wh, wm, wo, g, b, alpha, tm):
    n = x.shape[0]
    row = lambda w: pl.BlockSpec((tm, w), lambda i: (i, 0))
    consts = (wuv, wp, wh, wm, wo, g, b)
    return pl.pallas_call(
        functools.partial(_merge_kernel, alpha=alpha),
        grid=(n // tm,),
        in_specs=[row(D_MODEL), row(D_BR), row(D_BR), row(MLA_HEADS * KV_RANK), row(3 * D_MODEL)]
        + [_const_spec(a.shape) for a in consts],
        out_specs=row(D_MODEL),
        out_shape=jax.ShapeDtypeStruct((n, D_MODEL), F32),
        compiler_params=_cparams(("arbitrary",)),
        name="merge",
    )(x, y_pool, o_hg, o_lat, gates, *consts)


def _ffn_kernel(x_ref, wup_ref, cw_ref, cb_ref, wdn_ref, g_ref, b_ref, st_ref, y_ref, tail_ref, halo_ref,
                *, alpha, tm, seq_rows):
    x = x_ref[...]
    up = jnp.dot(x.astype(BF16), wup_ref[...], preferred_element_type=F32)
    r = lax.broadcasted_iota(jnp.int32, (tm, 1), 0)
    prev1 = pltpu.roll(up, 1, 0)
    prev2 = pltpu.roll(up, 2, 0)
    if seq_rows % tm == 0:
        i = pl.program_id(0)

        @pl.when(i % (seq_rows // tm) == 0)
        def _():
            halo_ref[...] = jnp.zeros(halo_ref.shape, F32)

        last = halo_ref[V7X_SUBLANES - 1:V7X_SUBLANES, :]
        last2 = halo_ref[V7X_SUBLANES - 2:V7X_SUBLANES - 1, :]
        prev1 = jnp.where(r == 0, last, prev1)
        prev2 = jnp.where(r == 0, last2, jnp.where(r == 1, last, prev2))
        halo_ref[...] = up[tm - V7X_SUBLANES:, :]
        tail_ref[0] = up[tm - V7X_SUBLANES:, :]
    else:
        n_s = tm // seq_rows
        t = r % seq_rows
        st = st_ref[...]
        wide = (n_s, seq_rows, 2 * D_FF)
        last = jnp.broadcast_to(st[:, 1:2, :], wide).reshape(tm, 2 * D_FF)
        last2 = jnp.broadcast_to(st[:, 0:1, :], wide).reshape(tm, 2 * D_FF)
        prev1 = jnp.where(t == 0, last, prev1)
        prev2 = jnp.where(t == 0, last2, jnp.where(t == 1, last, prev2))
        tail_ref[...] = up
    h = cb_ref[...] + prev2 * cw_ref[0:1, :] + prev1 * cw_ref[1:2, :] + up * cw_ref[2:3, :]
    gate = h[:, :D_FF]
    act = (gate * _sigmoid(gate) * h[:, D_FF:]).astype(BF16)
    f = jnp.dot(act, wdn_ref[...], preferred_element_type=F32)
    y_ref[...] = _layer_norm(alpha * x + f, g_ref[...], b_ref[...])


def _ffn(x, wup, cw, cb, wdn, g, b, state, alpha, tm, seq_rows):
    n = x.shape[0]
    row = pl.BlockSpec((tm, D_MODEL), lambda i: (i, 0))
    long_seq = seq_rows % tm == 0
    if long_seq:
        st_spec = _const_spec(state.shape)
        tail_spec = pl.BlockSpec((1, V7X_SUBLANES, 2 * D_FF), lambda i: (i, 0, 0))
        tail_shape = jax.ShapeDtypeStruct((n // tm, V7X_SUBLANES, 2 * D_FF), F32)
    else:
        n_s = tm // seq_rows
        st_spec = pl.BlockSpec((n_s, CONV_W - 1, 2 * D_FF), lambda i: (i, 0, 0))
        tail_spec = pl.BlockSpec((tm, 2 * D_FF), lambda i: (i, 0))
        tail_shape = jax.ShapeDtypeStruct((n, 2 * D_FF), F32)
    consts = (wup, cw, cb, wdn, g, b)
    return pl.pallas_call(
        functools.partial(_ffn_kernel, alpha=alpha, tm=tm, seq_rows=seq_rows),
        grid=(n // tm,),
        in_specs=[row] + [_const_spec(a.shape) for a in consts] + [st_spec],
        out_specs=[row, tail_spec],
        out_shape=[jax.ShapeDtypeStruct((n, D_MODEL), F32), tail_shape],
        scratch_shapes=[pltpu.VMEM((V7X_SUBLANES, 2 * D_FF), F32)],
        compiler_params=_cparams(("arbitrary",)),
        name="ffn",
    )(x, *consts, state)


def _pack_weights(w_in, w_uq, w_uk, w_uv):
    depth = w_in.shape[0]
    c_kpe = C_K1
    x1 = w_in[:, :, c_kpe:c_kpe + ROPE_HALF]
    x2 = w_in[:, :, c_kpe + ROPE_HALF:c_kpe + MLA_ROPE]
    pad = jnp.zeros((depth, D_MODEL, V7X_LANES - MLA_ROPE), w_in.dtype)
    wa = jnp.concatenate([w_in[:, :, :c_kpe], x1, x2, pad, x2, x1, pad, w_in[:, :, c_kpe + MLA_ROPE:]],
                         axis=-1).astype(BF16)
    uq = w_uq.reshape(depth, Q_RANK, MLA_HEADS, MLA_NOPE + MLA_ROPE)
    wq = jnp.concatenate([
        uq[..., :MLA_NOPE].reshape(depth, Q_RANK, -1),
        uq[..., MLA_NOPE:MLA_NOPE + ROPE_HALF].reshape(depth, Q_RANK, -1),
        uq[..., MLA_NOPE + ROPE_HALF:].reshape(depth, Q_RANK, -1)], axis=-1).astype(BF16)
    eye = jnp.eye(MLA_HEADS, dtype=w_uk.dtype)
    uk = jnp.transpose(w_uk, (0, 2, 3, 1))
    wuk = (eye[None, :, None, :, None] * uk[:, :, :, None, :]).reshape(
        depth, MLA_HEADS * MLA_NOPE, MLA_HEADS * KV_RANK).astype(BF16)
    uv = jnp.transpose(w_uv, (0, 2, 1, 3))
    wuv = (eye[None, :, None, :, None] * uv[:, :, :, None, :]).reshape(
        depth, MLA_HEADS * KV_RANK, MLA_HEADS * MLA_V).astype(BF16)
    return wa, wq, wuk, wuv


def _rope_tables(pos):
    inv = ROPE_THETA ** (-jnp.arange(ROPE_HALF, dtype=F32) / ROPE_HALF)
    ang = pos.astype(F32)[:, None] * inv[None, :]
    reps = V7X_LANES // ROPE_HALF
    return jnp.tile(jnp.cos(ang), (1, reps)), jnp.tile(jnp.sin(ang), (1, reps))


def _heads_q(q_lat, q_pe):
    n = q_lat.shape[0]
    pe = q_pe.reshape(n, 2, MLA_HEADS, ROPE_HALF).transpose(0, 2, 1, 3).reshape(n, MLA_HEADS, MLA_ROPE)
    return jnp.concatenate([q_lat.reshape(n, MLA_HEADS, KV_RANK), pe], axis=-1).reshape(n * MLA_HEADS, D_QK)


def _tiles(n_rows):
    return dict(in_proj=min(256, n_rows), merge=min(512, n_rows), ffn=min(256, n_rows))


POOL_TILE = 512
HGRN_TILE = 256
HGRN_CHUNK = 32
HGRN_SHORT_TILE = 128
ATTN_TQ = 256
ATTN_TK = 512


def _trunk(x_prompt, x_sample, cache_ckv, cache_kpe, state_pool, state_hgrn, state_conv, page_table,
           w_in, pool_w, pool_scale, hg_lower_bounds, hg_norm, q_norm, w_uq, kv_norm, w_uk, w_uv,
           w_br_pool, w_br_hg, w_br_mla, w_out, ln1_g, ln1_b, ffn_w_up, ffn_conv_w, ffn_conv_b,
           ffn_w_down, ln2_g, ln2_b):
    depth = w_in.shape[0]
    b, t, _ = x_prompt.shape
    sb, ts, _ = x_sample.shape
    past_len = page_table.shape[1] * cache_ckv.shape[2]
    alpha = (2 * depth) ** 0.25

    wa, wq, wuk, wuv = _pack_weights(w_in, w_uq, w_uk, w_uv)
    bf = lambda a: a.astype(BF16)
    pool_wb, wp, wh, wm, wo, wup, wdn = map(bf, (pool_w, w_br_pool, w_br_hg, w_br_mla, w_out, ffn_w_up,
                                                  ffn_w_down))
    row2 = lambda a: a.reshape(depth, 1, -1)
    pool_scale, hg_norm, q_norm, kv_norm, ln1_g, ln1_b, ln2_g, ln2_b, ffn_conv_b = map(
        row2, (pool_scale, hg_norm, q_norm, kv_norm, ln1_g, ln1_b, ln2_g, ln2_b, ffn_conv_b))

    np_, ns_ = b * t, sb * ts
    tp, tsm = _tiles(np_), _tiles(ns_)
    cos_p, sin_p = _rope_tables(jnp.arange(t, dtype=jnp.int32))
    cos_s, sin_s = _rope_tables(past_len + jnp.arange(tsm["in_proj"], dtype=jnp.int32) % ts)

    xp = x_prompt.reshape(np_, D_MODEL)
    xs = x_sample.reshape(ns_, D_MODEL)
    cache_kpe_t = jnp.swapaxes(cache_kpe, 2, 3)
    zero_state = jnp.zeros((b, N_SEG, SEG, SEG), F32)
    no_conv_state = jnp.zeros((1, CONV_W - 1, 2 * D_FF), F32)
    outs_p = [[] for _ in range(5)]
    outs_s = [[] for _ in range(5)]

    for l in range(depth):
        layer = jnp.full((1,), l, jnp.int32)

        (u, qa, kk, lf, vv, sg, q_lat, q_pe, ckv, kpe, kcat, gates) = _in_proj(
            layer, xp, wa[l], hg_lower_bounds, q_norm[l], kv_norm[l], wq[l], wuk[l], cos_p, sin_p,
            tp["in_proj"])
        y_pool = _pool_prompt(u.reshape(b, t, D_BR), pool_wb[l], pool_scale[l], min(POOL_TILE, t))
        o_hg, s_fin = _hgrn(qa, kk, lf, vv, sg, hg_norm[l], zero_state, b, t, min(HGRN_TILE, t),
                            min(HGRN_CHUNK, t))
        kcat = kcat.reshape(b, t, D_QK)
        o_lat = _attn_prompt(q_lat, q_pe, kcat, jnp.swapaxes(kcat[:, :, :KV_RANK], 1, 2),
                             min(ATTN_TQ, t), min(ATTN_TK, t))
        x1 = _merge(xp, y_pool.reshape(np_, D_BR), o_hg, o_lat, gates,
                    wuv[l], wp[l], wh[l], wm[l], wo[l], ln1_g[l], ln1_b[l], alpha, tp["merge"])
        xp, tail = _ffn(x1, wup[l], ffn_conv_w[l], ffn_conv_b[l], wdn[l], ln2_g[l], ln2_b[l],
                        no_conv_state, alpha, tp["ffn"], t)
        tiles_per_seq = t // tp["ffn"]
        conv_p = tail.reshape(b, tiles_per_seq, V7X_SUBLANES, 2 * D_FF)[:, -1, -(CONV_W - 1):, :]
        for lst, v in zip(outs_p, (ckv.reshape(b, t, KV_RANK), kpe.reshape(b, t, MLA_ROPE),
                                   u.reshape(b, t, D_BR)[:, -POOL_BUF:], s_fin, conv_p)):
            lst.append(v)

        (u, qa, kk, lf, vv, sg, q_lat, q_pe, ckv, kpe, _, gates) = _in_proj(
            layer, xs, wa[l], hg_lower_bounds, q_norm[l], kv_norm[l], wq[l], wuk[l], cos_s, sin_s,
            tsm["in_proj"])
        ext = jnp.concatenate([state_pool[l], u.reshape(sb, ts, D_BR)], axis=1)
        y_pool = _pool_sample(jnp.swapaxes(ext, 0, 1), pool_wb[l], pool_scale[l], ts, past_len)
        y_pool = jnp.swapaxes(y_pool, 0, 1).reshape(ns_, D_BR)
        o_hg, s_fin = _hgrn(qa, kk, lf, vv, sg, hg_norm[l], state_hgrn[l], sb, ts,
                            min(HGRN_SHORT_TILE, ns_), ts)
        k_new = jnp.concatenate([ckv, kpe], axis=-1).reshape(sb, ts, D_QK)
        o_lat = _attn_sample(layer, page_table, _heads_q(q_lat, q_pe).reshape(sb, ts * MLA_HEADS, D_QK),
                             k_new, ckv.reshape(sb, ts, KV_RANK), cache_ckv, cache_kpe_t)
        x1 = _merge(xs, y_pool, o_hg, o_lat.reshape(ns_, MLA_HEADS * KV_RANK), gates,
                    wuv[l], wp[l], wh[l], wm[l], wo[l], ln1_g[l], ln1_b[l], alpha, tsm["merge"])
        xs, up = _ffn(x1, wup[l], ffn_conv_w[l], ffn_conv_b[l], wdn[l], ln2_g[l], ln2_b[l],
                      state_conv[l], alpha, tsm["ffn"], ts)
        conv_s = jnp.concatenate([state_conv[l], up.reshape(sb, ts, 2 * D_FF)], axis=1)[:, -(CONV_W - 1):]
        for lst, v in zip(outs_s, (ckv.reshape(sb, ts, KV_RANK), kpe.reshape(sb, ts, MLA_ROPE),
                                   ext[:, -POOL_BUF:], s_fin, conv_s)):
            lst.append(v)

    stack = lambda lists: [jnp.stack(v) for v in lists]
    return (xp.reshape(b, t, D_MODEL), xs.reshape(sb, ts, D_MODEL), *stack(outs_p), *stack(outs_s))


def kernel(x_prompt, x_sample, cache_ckv, cache_kpe, state_pool, state_hgrn, state_conv, page_table, w_in, pool_w, pool_scale, hg_lower_bounds, hg_norm, q_norm, w_uq, kv_norm, w_uk, w_uv, w_br_pool, w_br_hg, w_br_mla, w_out, ln1_g, ln1_b, ffn_w_up, ffn_conv_w, ffn_conv_b, ffn_w_down, ln2_g, ln2_b):
    return _trunk(x_prompt, x_sample, cache_ckv, cache_kpe, state_pool, state_hgrn, state_conv, page_table,
                  w_in, pool_w, pool_scale, hg_lower_bounds, hg_norm, q_norm, w_uq, kv_norm, w_uk, w_uv,
                  w_br_pool, w_br_hg, w_br_mla, w_out, ln1_g, ln1_b, ffn_w_up, ffn_conv_w, ffn_conv_b,
                  ffn_w_down, ln2_g, ln2_b)
```

```python
import functools

import jax
import jax.numpy as jnp
from jax import lax
from jax.experimental import pallas as pl
from jax.experimental.pallas import tpu as pltpu

F32 = jnp.float32
BF16 = jnp.bfloat16

D_MODEL = 1024
POOL_WINDOWS = (2, 4, 8, 16)
SEG = 128
N_SEG = 4
D_BR = N_SEG * SEG
POOL_BUF = max(POOL_WINDOWS) - 1
MLA_HEADS = 8
MLA_NOPE = 64
MLA_ROPE = 32
ROPE_HALF = MLA_ROPE // 2
MLA_V = 64
Q_RANK = 256
KV_RANK = SEG
D_QK = KV_RANK + MLA_ROPE
VT_ROWS = KV_RANK + 16
D_FF = 2816
CONV_W = 3
MLA_SCALE = (MLA_NOPE + MLA_ROPE) ** -0.5
LOG2_E = 1.4426950408889634
Q_SCALE = MLA_SCALE * LOG2_E
ROPE_THETA = 10000.0
LN_EPS = 1e-5
RMS_EPS = 1e-6

V7X_VMEM_BYTES = 64 * 1024 * 1024
V7X_LANES = 128
V7X_SUBLANES = 8
VMEM_LIMIT = V7X_VMEM_BYTES - 6 * 1024 * 1024

C_POOL, C_HQ, C_HF, C_HI, C_HG, C_CQ, C_CKV, C_K1, C_K2, C_END = (
    0, 512, 1024, 1536, 2048, 2560, 2816, 2944, 3072, 3200)
C_KPE_SRC = 2944
C_GATE_SRC = C_KPE_SRC + MLA_ROPE

NEG_BIG = -1e30
HGRN_SAFE_LOG_DECAY = -60.0


def _cparams(sem):
    return pltpu.CompilerParams(dimension_semantics=sem, vmem_limit_bytes=VMEM_LIMIT)


def _const_spec(shape):
    nd = len(shape)
    return pl.BlockSpec(shape, lambda *_: (0,) * nd, pipeline_mode=pl.Buffered(1))


def _sigmoid(x):
    return 0.5 * (jnp.tanh(0.5 * x) + 1.0)


def _silu(x):
    half = 0.5 * x
    return half + half * jnp.tanh(half)


def _layer_norm(h, g, b):
    mu = jnp.mean(h, axis=-1, keepdims=True)
    d = h - mu
    var = jnp.mean(d * d, axis=-1, keepdims=True)
    return d * lax.rsqrt(var + LN_EPS) * g + b


def _in_proj_kernel(layer_ref, x_ref, w_ref, hglb_ref, qn_ref, kvn_ref, wq_ref, wuk_ref, cos_ref, sin_ref,
                    u_ref, qa_ref, kk_ref, lf_ref, vv_ref, sg_ref, qlat_ref, qpe_ref, ckv_ref, kpe_ref,
                    kcat_ref):
    xb = x_ref[...].astype(BF16)

    def seg(lo, hi):
        return jnp.dot(xb, w_ref[:, lo:hi], preferred_element_type=F32)

    u_ref[...] = seg(C_POOL, C_HQ)

    qa_ref[...] = _silu(seg(C_HQ, C_HF))

    layer = layer_ref[0]
    hb = hglb_ref[...]
    e = jnp.exp(hb - jnp.max(hb, axis=0, keepdims=True))
    row = lax.broadcasted_iota(jnp.int32, hb.shape, 0)
    lb = (jnp.sum(jnp.where((row >= 1) & (row <= layer), e, 0.0), axis=0, keepdims=True)
          / jnp.sum(e, axis=0, keepdims=True))

    fl = seg(C_HF, C_HI)
    log_sig = jnp.minimum(fl, 0.0) - jnp.log(1.0 + jnp.exp(-jnp.abs(fl)))
    a = jnp.log(lb)
    b = jnp.log1p(-lb) + log_sig
    lf_ref[...] = jnp.maximum(a, b) + jnp.log(1.0 + jnp.exp(-jnp.abs(a - b)))
    kk_ref[...] = (1.0 - lb) * _sigmoid(-fl)

    vv_ref[...] = seg(C_HI, C_HG)
    sg_ref[...] = _silu(seg(C_HG, C_CQ))

    cq = seg(C_CQ, C_CKV)
    cqn = cq * lax.rsqrt(jnp.mean(cq * cq, axis=-1, keepdims=True) + RMS_EPS) * qn_ref[...]
    q = jnp.dot(cqn.astype(BF16), wq_ref[...], preferred_element_type=F32)
    cos = cos_ref[...]
    sin = sin_ref[...]
    n_nope = MLA_HEADS * MLA_NOPE
    x1 = q[:, n_nope:n_nope + V7X_LANES]
    x2 = q[:, n_nope + V7X_LANES:]
    qpe_ref[:, :V7X_LANES] = ((x1 * cos - x2 * sin) * Q_SCALE).astype(BF16)
    qpe_ref[:, V7X_LANES:] = ((x1 * sin + x2 * cos) * Q_SCALE).astype(BF16)
    qlat = jnp.dot(q[:, :n_nope].astype(BF16), wuk_ref[...], preferred_element_type=F32)
    qlat_ref[...] = (qlat * Q_SCALE).astype(BF16)

    ckv = seg(C_CKV, C_K1)
    ckv = ckv * lax.rsqrt(jnp.mean(ckv * ckv, axis=-1, keepdims=True) + RMS_EPS) * kvn_ref[...]
    ckv_ref[...] = ckv
    kcat_ref[:, :KV_RANK] = ckv.astype(BF16)

    k1 = seg(C_K1, C_K2)
    k2 = seg(C_K2, C_END)
    lane = lax.broadcasted_iota(jnp.int32, sin.shape, 1)
    kr = k1 * cos + k2 * jnp.where(lane < ROPE_HALF, -sin, sin)
    kpe_ref[...] = kr[:, :MLA_ROPE]
    kcat_ref[:, KV_RANK:] = kr[:, :MLA_ROPE].astype(BF16)


def _in_proj(layer, x, wa, hglb, q_norm, kv_norm, wq, wuk, cos_tab, sin_tab, tm):
    n = x.shape[0]
    tab_blocks = cos_tab.shape[0] // tm
    row = lambda w: pl.BlockSpec((tm, w), lambda i, *_: (i, 0))
    tab = pl.BlockSpec((tm, V7X_LANES), lambda i, *_: (i % tab_blocks, 0))
    widths = (D_BR,) * 6 + (MLA_HEADS * KV_RANK, MLA_HEADS * MLA_ROPE, KV_RANK, MLA_ROPE, D_QK)
    dtypes = (F32,) * 6 + (BF16, BF16, F32, F32, BF16)
    grid_spec = pltpu.PrefetchScalarGridSpec(
        num_scalar_prefetch=1,
        grid=(n // tm,),
        in_specs=[row(D_MODEL), _const_spec(wa.shape), _const_spec(hglb.shape), _const_spec(q_norm.shape),
                  _const_spec(kv_norm.shape), _const_spec(wq.shape), _const_spec(wuk.shape), tab, tab],
        out_specs=[row(w) for w in widths],
    )
    return pl.pallas_call(
        _in_proj_kernel,
        grid_spec=grid_spec,
        out_shape=[jax.ShapeDtypeStruct((n, w), d) for w, d in zip(widths, dtypes)],
        compiler_params=_cparams(("arbitrary",)),
        name="in_proj",
    )(layer, x, wa, hglb, q_norm, kv_norm, wq, wuk, cos_tab, sin_tab)


def _pool_prompt_kernel(u_ref, w_ref, sc_ref, y_ref, ext_ref, *, tt):
    halo = 2 * V7X_SUBLANES
    j = pl.program_id(1)

    @pl.when(j == 0)
    def _():
        ext_ref[0:halo, :] = jnp.zeros((halo, D_BR), F32)

    u = u_ref[0]
    ext_ref[halo:halo + tt, :] = u
    pos = j * tt + lax.broadcasted_iota(jnp.int32, (tt, 1), 0)
    for g, win in enumerate(POOL_WINDOWS):
        c0 = g * SEG
        acc = u[:, c0:c0 + SEG]
        for d in range(1, win):
            acc = acc + ext_ref[halo - d:halo - d + tt, c0:c0 + SEG]
        cnt = jnp.minimum(pos + 1, win).astype(F32)
        diff = (acc / cnt - u[:, c0:c0 + SEG]).astype(BF16)
        y = jnp.dot(diff, w_ref[g], preferred_element_type=F32) * sc_ref[:, c0:c0 + SEG]
        y_ref[0, :, c0:c0 + SEG] = y.astype(BF16)
    ext_ref[0:halo, :] = ext_ref[tt:tt + halo, :]


def _pool_prompt(u, w_pool, scale, tt):
    b, t, _ = u.shape
    return pl.pallas_call(
        functools.partial(_pool_prompt_kernel, tt=tt),
        grid=(b, t // tt),
        in_specs=[pl.BlockSpec((1, tt, D_BR), lambda i, j: (i, j, 0)), _const_spec(w_pool.shape),
                  _const_spec(scale.shape)],
        out_specs=pl.BlockSpec((1, tt, D_BR), lambda i, j: (i, j, 0)),
        out_shape=jax.ShapeDtypeStruct((b, t, D_BR), BF16),
        scratch_shapes=[pltpu.VMEM((tt + 2 * V7X_SUBLANES, D_BR), F32)],
        compiler_params=_cparams(("arbitrary", "arbitrary")),
        name="pool_prompt",
    )(u, w_pool, scale)


def _pool_sample_kernel(ext_ref, w_ref, sc_ref, y_ref, *, t_new, past_len):
    s = ext_ref.shape[1]
    for g, win in enumerate(POOL_WINDOWS):
        c0 = g * SEG
        diffs = []
        for t in range(t_new):
            cur = ext_ref[POOL_BUF + t, :, c0:c0 + SEG]
            acc = cur
            for d in range(1, win):
                acc = acc + ext_ref[POOL_BUF + t - d, :, c0:c0 + SEG]
            cnt = float(min(past_len + t + 1, win))
            diffs.append(acc / cnt - cur)
        diff = jnp.concatenate(diffs, axis=0).astype(BF16)
        y = jnp.dot(diff, w_ref[g], preferred_element_type=F32) * sc_ref[:, c0:c0 + SEG]
        y_ref[:, :, c0:c0 + SEG] = y.reshape(t_new, s, SEG).astype(BF16)


def _pool_sample(ext_tm, w_pool, scale, t_new, past_len):
    _, s, _ = ext_tm.shape
    return pl.pallas_call(
        functools.partial(_pool_sample_kernel, t_new=t_new, past_len=past_len),
        grid=(1,),
        in_specs=[_const_spec(ext_tm.shape), _const_spec(w_pool.shape), _const_spec(scale.shape)],
        out_specs=pl.BlockSpec((t_new, s, D_BR), lambda i: (0, 0, 0)),
        out_shape=jax.ShapeDtypeStruct((t_new, s, D_BR), BF16),
        compiler_params=_cparams(("arbitrary",)),
        name="pool_sample",
    )(ext_tm, w_pool, scale)


def _hgrn_kernel(qa_ref, kk_ref, lf_ref, vv_ref, sg_ref, nw_ref, s0_ref, o_ref, sfin_ref,
                 st_ref, lsc_ref, ksc_ref, *, c, n_chunks, chained):
    tile = c * n_chunks
    log2c = c.bit_length() - 1
    j = pl.program_id(1)

    if chained:
        @pl.when(j == 0)
        def _():
            for h in range(N_SEG):
                st_ref[h] = s0_ref[0, h].T

    r_i = lax.broadcasted_iota(jnp.int32, (tile, tile), 0)
    c_i = lax.broadcasted_iota(jnp.int32, (tile, tile), 1)
    chunk_of_row = jnp.right_shift(r_i, log2c)
    blk_causal = (chunk_of_row == jnp.right_shift(c_i, log2c)) & (r_i >= c_i)
    tri = blk_causal.astype(BF16)

    chunk_of_col = jnp.right_shift(lax.broadcasted_iota(jnp.int32, (SEG, tile), 1), log2c)

    def nt(a, b):
        return lax.dot_general(a, b, (((1,), (1,)), ((), ())), preferred_element_type=F32)

    def per_chunk(x3):
        return jnp.broadcast_to(x3, (n_chunks, c, D_BR)).reshape(tile, D_BR)

    lf = lf_ref[...]
    hi = lf.astype(BF16)
    r1 = lf - hi.astype(F32)
    mid = r1.astype(BF16)
    lo = (r1 - mid.astype(F32)).astype(BF16)
    cum = (jnp.dot(tri, hi, preferred_element_type=F32) + jnp.dot(tri, mid, preferred_element_type=F32)
           + jnp.dot(tri, lo, preferred_element_type=F32))
    cum3 = cum.reshape(n_chunks, c, D_BR)
    tot = per_chunk(cum3[:, c - 1:c, :])
    qa = qa_ref[...]
    kk = kk_ref[...]
    vv = vv_ref[...]
    q_in = qa * jnp.exp(cum)
    k_out = kk * jnp.exp(tot - cum)
    dec = jnp.exp(tot)
    safe = jnp.min(cum) >= HGRN_SAFE_LOG_DECAY

    def fast():
        k_rel = kk * jnp.exp(-cum)
        return [nt(q_in[:, h * SEG:(h + 1) * SEG].astype(BF16), k_rel[:, h * SEG:(h + 1) * SEG].astype(BF16))
                for h in range(N_SEG)]

    def exact():
        lsc_ref[...] = cum3
        ksc_ref[...] = kk.reshape(n_chunks, c, D_BR)

        def col(s, accs):
            w = jnp.exp(jnp.minimum(cum - per_chunk(lsc_ref[:, pl.ds(s, 1), :]), 0.0))
            prod = qa * per_chunk(ksc_ref[:, pl.ds(s, 1), :]) * w
            here = c_i == chunk_of_row * c + s
            return [jnp.where(here, jnp.sum(prod[:, h * SEG:(h + 1) * SEG], axis=-1, keepdims=True), accs[h])
                    for h in range(N_SEG)]

        return lax.fori_loop(0, c, col, [jnp.zeros((tile, tile), F32)] * N_SEG)

    a_all = lax.cond(safe, fast, exact)

    for h in range(N_SEG):
        cols = slice(h * SEG, (h + 1) * SEG)
        a_h = jnp.where(blk_causal, a_all[h], 0.0).astype(BF16)
        o_intra = jnp.dot(a_h, vv[:, cols].astype(BF16), preferred_element_type=F32)
        vt = vv[:, cols].T
        vexp_t = jnp.concatenate([jnp.where(chunk_of_col == ci, vt, 0.0) for ci in range(n_chunks)],
                                 axis=0).astype(BF16)
        kv_all = jnp.dot(vexp_t, k_out[:, cols].astype(BF16), preferred_element_type=F32)
        states = []
        st = st_ref[h] if chained else None
        for ci in range(n_chunks):
            if not chained:
                st = s0_ref[ci, h].T
            states.append(st)
            st = st * dec[ci * c:ci * c + 1, cols] + kv_all[ci * SEG:(ci + 1) * SEG]
            if not chained:
                sfin_ref[ci, h] = st.T
        if chained:
            st_ref[h] = st
        p_all = nt(q_in[:, cols].astype(BF16), jnp.concatenate(states, axis=0).astype(BF16))
        o_inter = jnp.concatenate([p_all[ci * c:(ci + 1) * c, ci * SEG:(ci + 1) * SEG]
                                   for ci in range(n_chunks)], axis=0)
        o = o_intra + o_inter
        on = o * lax.rsqrt(jnp.mean(o * o, axis=-1, keepdims=True) + RMS_EPS) * nw_ref[:, cols]
        o_ref[:, cols] = (on * sg_ref[:, cols]).astype(BF16)

    if chained:
        @pl.when(j == pl.num_programs(1) - 1)
        def _():
            for h in range(N_SEG):
                sfin_ref[0, h] = st_ref[h].T


def _hgrn(qa, kk, lf, vv, sg, norm_w, state0, n_seq, t_len, tile, c):
    n = qa.shape[0]
    assert c & (c - 1) == 0 and tile % c == 0
    n_chunks = tile // c
    chained = t_len >= tile
    if chained:
        steps = t_len // tile
        grid = (n_seq, steps)
        row = pl.BlockSpec((tile, D_BR), lambda i, j: (i * steps + j, 0))
        st_spec = pl.BlockSpec((1, N_SEG, SEG, SEG), lambda i, j: (i, 0, 0, 0))
    else:
        assert t_len == c and n_seq % n_chunks == 0
        grid = (n_seq // n_chunks, 1)
        row = pl.BlockSpec((tile, D_BR), lambda i, j: (i, 0))
        st_spec = pl.BlockSpec((n_chunks, N_SEG, SEG, SEG), lambda i, j: (i, 0, 0, 0))
    return pl.pallas_call(
        functools.partial(_hgrn_kernel, c=c, n_chunks=n_chunks, chained=chained),
        grid=grid,
        in_specs=[row, row, row, row, row, _const_spec(norm_w.shape), st_spec],
        out_specs=[row, st_spec],
        out_shape=[jax.ShapeDtypeStruct((n, D_BR), BF16),
                   jax.ShapeDtypeStruct((n_seq, N_SEG, SEG, SEG), F32)],
        scratch_shapes=[pltpu.VMEM((N_SEG, SEG, SEG), F32), pltpu.VMEM((n_chunks, c, D_BR), F32),
                        pltpu.VMEM((n_chunks, c, D_BR), F32)],
        compiler_params=_cparams(("arbitrary", "arbitrary")),
        name="hgrn",
    )(qa, kk, lf, vv, sg, norm_w, state0)


def _attn_prompt_kernel(ql_ref, qp_ref, k_ref, vt_ref, o_ref, qt_ref, m_ref, acc_ref, *, tq, tk):
    qi = pl.program_id(1)
    cols = tq * MLA_HEADS

    qpt = qp_ref[...].astype(F32).T
    for h in range(MLA_HEADS):
        cs = slice(h * tq, (h + 1) * tq)
        qt_ref[0:KV_RANK, cs] = ql_ref[:, h * KV_RANK:(h + 1) * KV_RANK].astype(F32).T.astype(BF16)
        qt_ref[KV_RANK:KV_RANK + ROPE_HALF, cs] = qpt[h * ROPE_HALF:(h + 1) * ROPE_HALF].astype(BF16)
        qt_ref[KV_RANK + ROPE_HALF:D_QK, cs] = qpt[V7X_LANES + h * ROPE_HALF:
                                                  V7X_LANES + (h + 1) * ROPE_HALF].astype(BF16)

    m_ref[...] = jnp.full((1, cols), NEG_BIG, F32)
    acc_ref[...] = jnp.zeros(acc_ref.shape, F32)

    def step(ks, masked):
        s = jnp.dot(k_ref[0, pl.ds(ks, tk), :], qt_ref[...], preferred_element_type=F32)
        if masked:
            key = ks + lax.broadcasted_iota(jnp.int32, (tk, cols), 0)
            tok = qi * tq + (lax.broadcasted_iota(jnp.int32, (tk, cols), 1) & (tq - 1))
            s = jnp.where(key <= tok, s, NEG_BIG)
        m_old = m_ref[...]
        m_new = jnp.maximum(m_old, jnp.max(s, axis=0, keepdims=True))
        p = jnp.exp2(s - m_new).astype(BF16)
        acc_ref[...] = jnp.exp2(m_old - m_new) * acc_ref[...] + jnp.dot(
            vt_ref[0, :, pl.ds(ks, tk)], p, preferred_element_type=F32)
        m_ref[...] = m_new

    n_full = (qi * tq) // tk

    def body(jb, carry):
        step(pl.multiple_of(jb * tk, tk), False)
        return carry

    lax.fori_loop(0, n_full, body, 0)
    step(pl.multiple_of(n_full * tk, tk), True)
    o = acc_ref[0:KV_RANK, :] / acc_ref[KV_RANK:KV_RANK + 1, :]
    for h in range(MLA_HEADS):
        o_ref[:, h * KV_RANK:(h + 1) * KV_RANK] = o[:, h * tq:(h + 1) * tq].T.astype(BF16)


def _attn_prompt(q_lat, q_pe, k, vt, tq, tk):
    b, t, _ = k.shape
    assert t % tk == 0 and tk % tq == 0 and tq & (tq - 1) == 0
    nq = t // tq
    cols = tq * MLA_HEADS
    qrow = lambda w: pl.BlockSpec((tq, w), lambda i, j: (i * nq + j, 0))
    return pl.pallas_call(
        functools.partial(_attn_prompt_kernel, tq=tq, tk=tk),
        grid=(b, nq),
        in_specs=[qrow(MLA_HEADS * KV_RANK), qrow(MLA_HEADS * MLA_ROPE),
                  pl.BlockSpec((1, t, D_QK), lambda i, j: (i, 0, 0)),
                  pl.BlockSpec((1, VT_ROWS, t), lambda i, j: (i, 0, 0))],
        out_specs=qrow(MLA_HEADS * KV_RANK),
        out_shape=jax.ShapeDtypeStruct((b * t, MLA_HEADS * KV_RANK), BF16),
        scratch_shapes=[pltpu.VMEM((D_QK, cols), BF16), pltpu.VMEM((1, cols), F32),
                        pltpu.VMEM((VT_ROWS, cols), F32)],
        compiler_params=_cparams(("arbitrary", "arbitrary")),
        name="attn_prompt",
    )(q_lat, q_pe, k, vt)


def _attn_sample_kernel(layer_ref, pt_ref, q_ref, kn_ref, vn_ref, ckv_hbm, kpe_hbm, o_ref,
                        cbuf, pbuf, sem, *, n_pages, page, t_new):
    b = pl.program_id(0)
    nb = pl.num_programs(0)
    slot = b % 2
    layer = layer_ref[0]

    def page_copies(seq, slot_, p):
        pg = pt_ref[seq, p]
        dst = pl.ds(p * page, page)
        return (pltpu.make_async_copy(ckv_hbm.at[layer, pg], cbuf.at[slot_, dst, :], sem.at[0, slot_]),
                pltpu.make_async_copy(kpe_hbm.at[layer, pg], pbuf.at[slot_, :, dst], sem.at[1, slot_]))

    def start_all(seq, slot_):
        for p in range(n_pages):
            for cp in page_copies(seq, slot_, p):
                cp.start()

    @pl.when(b == 0)
    def _():
        start_all(0, 0)

    @pl.when(b + 1 < nb)
    def _():
        start_all(b + 1, 1 - slot)

    for p in range(n_pages):
        for cp in page_copies(b, slot, p):
            cp.wait()

    q = q_ref[0]
    nt = (((1,), (1,)), ((), ()))
    n_ch = SAMPLE_KEY_CHUNKS if n_pages % SAMPLE_KEY_CHUNKS == 0 else 1
    ck = n_pages * page // n_ch
    parts = []
    for i in range(n_ch):
        kc = cbuf[slot, i * ck:(i + 1) * ck, :].astype(BF16)
        s = (lax.dot_general(q[:, :KV_RANK], kc, nt, preferred_element_type=F32)
             + jnp.dot(q[:, KV_RANK:], pbuf[slot, :, i * ck:(i + 1) * ck].astype(BF16),
                       preferred_element_type=F32))
        m_i = jnp.max(s, axis=-1, keepdims=True)
        p = jnp.exp2(s - m_i)
        parts.append((m_i, jnp.sum(p, axis=-1, keepdims=True),
                      jnp.dot(p.astype(BF16), kc, preferred_element_type=F32)))
    s_new = lax.dot_general(q.astype(F32), kn_ref[0], nt, preferred_element_type=F32)
    rows = t_new * MLA_HEADS
    tok = lax.broadcasted_iota(jnp.int32, (rows, t_new), 0) // MLA_HEADS
    key = lax.broadcasted_iota(jnp.int32, (rows, t_new), 1)
    s_new = jnp.where(key <= tok, s_new, NEG_BIG)
    m_n = jnp.max(s_new, axis=-1, keepdims=True)
    p_new = jnp.exp2(s_new - m_n)
    parts.append((m_n, jnp.sum(p_new, axis=-1, keepdims=True),
                  jnp.dot(p_new, vn_ref[0], preferred_element_type=F32)))
    m = functools.reduce(jnp.maximum, [pt[0] for pt in parts])
    w = [jnp.exp2(pt[0] - m) for pt in parts]
    denom = sum(wi * pt[1] for wi, pt in zip(w, parts))
    o = sum(wi * pt[2] for wi, pt in zip(w, parts))
    o_ref[0] = (o / denom).astype(BF16)


def _attn_sample(layer, page_table, q, k_new, v_new, cache_ckv, cache_kpe_t):
    n_seq, n_pages = page_table.shape
    page = cache_ckv.shape[2]
    t_new = k_new.shape[1]
    rows = t_new * MLA_HEADS
    past = n_pages * page
    grid_spec = pltpu.PrefetchScalarGridSpec(
        num_scalar_prefetch=2,
        grid=(n_seq,),
        in_specs=[pl.BlockSpec((1, rows, D_QK), lambda i, *_: (i, 0, 0)),
                  pl.BlockSpec((1, t_new, D_QK), lambda i, *_: (i, 0, 0)),
                  pl.BlockSpec((1, t_new, KV_RANK), lambda i, *_: (i, 0, 0)),
                  pl.BlockSpec(memory_space=pl.ANY), pl.BlockSpec(memory_space=pl.ANY)],
        out_specs=pl.BlockSpec((1, rows, KV_RANK), lambda i, *_: (i, 0, 0)),
        scratch_shapes=[pltpu.VMEM((2, past, KV_RANK), F32), pltpu.VMEM((2, MLA_ROPE, past), F32),
                        pltpu.SemaphoreType.DMA((2, 2))],
    )
    return pl.pallas_call(
        functools.partial(_attn_sample_kernel, n_pages=n_pages, page=page, t_new=t_new),
        grid_spec=grid_spec,
        out_shape=jax.ShapeDtypeStruct((n_seq, rows, KV_RANK), BF16),
        compiler_params=_cparams(("arbitrary",)),
        name="attn_sample",
    )(layer, page_table, q, k_new, v_new, cache_ckv, cache_kpe_t)


def _merge_kernel(x_ref, yp_ref, oh_ref, ol_ref, wg_ref, wuv_ref, wp_ref, wh_ref, wm_ref, wo_ref,
                  g_ref, b_ref, y_ref, *, alpha):
    dot = functools.partial(jnp.dot, preferred_element_type=F32)
    x = x_ref[...]
    xb = x.astype(BF16)

    def gate(i):
        return _sigmoid(dot(xb, wg_ref[:, i * D_MODEL:(i + 1) * D_MODEL]))

    o_mla = dot(ol_ref[...], wuv_ref[...]).astype(BF16)
    merged = (gate(0) * dot(yp_ref[...], wp_ref[...]) + gate(1) * dot(oh_ref[...], wh_ref[...])
              + gate(2) * dot(o_mla, wm_ref[...]))
    h = alpha * x + dot(merged.astype(BF16), wo_ref[...])
    y_ref[...] = _layer_norm(h, g_ref[...], b_ref[...])


def _merge(x, y_pool, o_hg, o_lat, wg, wuv, wp, wh, wm, wo, g, b, alpha, tm):
    n = x.shape[0]
    row = lambda w: pl.BlockSpec((tm, w), lambda i: (i, 0))
    consts = (wg, wuv, wp, wh, wm, wo, g, b)
    return pl.pallas_call(
        functools.partial(_merge_kernel, alpha=alpha),
        grid=(n // tm,),
        in_specs=[row(D_MODEL), row(D_BR), row(D_BR), row(MLA_HEADS * KV_RANK)]
        + [_const_spec(a.shape) for a in consts],
        out_specs=row(D_MODEL),
        out_shape=jax.ShapeDtypeStruct((n, D_MODEL), F32),
        compiler_params=_cparams(("arbitrary",)),
        name="merge",
    )(x, y_pool, o_hg, o_lat, *consts)


def _ffn_kernel(x_ref, wup_ref, cw_ref, cb_ref, wdn_ref, g_ref, b_ref, st_ref, y_ref, tail_ref, halo_ref,
                act_ref, *, alpha, tm, seq_rows):
    x = x_ref[...]
    xb = x.astype(BF16)
    long_seq = seq_rows % tm == 0
    sub = V7X_SUBLANES
    if long_seq:
        @pl.when(pl.program_id(0) % (seq_rows // tm) == 0)
        def _():
            halo_ref[...] = jnp.zeros(halo_ref.shape, F32)

        r8 = lax.broadcasted_iota(jnp.int32, (sub, 1), 0)
    else:
        n_s = tm // seq_rows
        t = lax.broadcasted_iota(jnp.int32, (tm, 1), 0) % seq_rows

    def conv_cols(lo):
        cs = slice(lo, lo + FFN_COL_BLOCK)
        up = jnp.dot(xb, wup_ref[:, cs], preferred_element_type=F32)
        prev1 = pltpu.roll(up, 1, 0)
        prev2 = pltpu.roll(up, 2, 0)
        if long_seq:
            last = halo_ref[sub - 1:sub, cs]
            last2 = halo_ref[sub - 2:sub - 1, cs]
            head1 = jnp.where(r8 == 0, last, prev1[:sub])
            head2 = jnp.where(r8 == 0, last2, jnp.where(r8 == 1, last, prev2[:sub]))
            prev1 = jnp.concatenate([head1, prev1[sub:]], axis=0)
            prev2 = jnp.concatenate([head2, prev2[sub:]], axis=0)
            halo_ref[:, cs] = up[tm - sub:, :]
            tail_ref[0, :, cs] = up[tm - sub:, :]
        else:
            st = st_ref[:, :, cs]
            wide = (n_s, seq_rows, FFN_COL_BLOCK)
            last = jnp.broadcast_to(st[:, 1:2, :], wide).reshape(tm, FFN_COL_BLOCK)
            last2 = jnp.broadcast_to(st[:, 0:1, :], wide).reshape(tm, FFN_COL_BLOCK)
            prev1 = jnp.where(t == 0, last, prev1)
            prev2 = jnp.where(t == 0, last2, jnp.where(t == 1, last, prev2))
            tail_ref[:, cs] = up
        return cb_ref[:, cs] + prev2 * cw_ref[0:1, cs] + prev1 * cw_ref[1:2, cs] + up * cw_ref[2:3, cs]

    for jb in range(D_FF // FFN_COL_BLOCK):
        lo = jb * FFN_COL_BLOCK
        act_ref[:, lo:lo + FFN_COL_BLOCK] = (_silu(conv_cols(lo)) * conv_cols(D_FF + lo)).astype(BF16)
    f = jnp.dot(act_ref[...], wdn_ref[...], preferred_element_type=F32)
    y_ref[...] = _layer_norm(alpha * x + f, g_ref[...], b_ref[...])


def _ffn(x, wup, cw, cb, wdn, g, b, state, alpha, tm, seq_rows):
    n = x.shape[0]
    row = pl.BlockSpec((tm, D_MODEL), lambda i: (i, 0))
    long_seq = seq_rows % tm == 0
    if long_seq:
        st_spec = _const_spec(state.shape)
        tail_spec = pl.BlockSpec((1, V7X_SUBLANES, 2 * D_FF), lambda i: (i, 0, 0))
        tail_shape = jax.ShapeDtypeStruct((n // tm, V7X_SUBLANES, 2 * D_FF), F32)
    else:
        n_s = tm // seq_rows
        st_spec = pl.BlockSpec((n_s, CONV_W - 1, 2 * D_FF), lambda i: (i, 0, 0))
        tail_spec = pl.BlockSpec((tm, 2 * D_FF), lambda i: (i, 0))
        tail_shape = jax.ShapeDtypeStruct((n, 2 * D_FF), F32)
    consts = (wup, cw, cb, wdn, g, b)
    return pl.pallas_call(
        functools.partial(_ffn_kernel, alpha=alpha, tm=tm, seq_rows=seq_rows),
        grid=(n // tm,),
        in_specs=[row] + [_const_spec(a.shape) for a in consts] + [st_spec],
        out_specs=[row, tail_spec],
        out_shape=[jax.ShapeDtypeStruct((n, D_MODEL), F32), tail_shape],
        scratch_shapes=[pltpu.VMEM((V7X_SUBLANES, 2 * D_FF), F32), pltpu.VMEM((tm, D_FF), BF16)],
        compiler_params=_cparams(("arbitrary",)),
        name="ffn",
    )(x, *consts, state)


def _pack_weights(w_in, w_uq, w_uk, w_uv):
    depth = w_in.shape[0]
    x1 = w_in[:, :, C_KPE_SRC:C_KPE_SRC + ROPE_HALF]
    x2 = w_in[:, :, C_KPE_SRC + ROPE_HALF:C_GATE_SRC]
    pad = jnp.zeros((depth, D_MODEL, V7X_LANES - MLA_ROPE), w_in.dtype)
    wa = jnp.concatenate([w_in[:, :, :C_KPE_SRC], x1, x2, pad, x2, x1, pad], axis=-1).astype(BF16)
    wg = w_in[:, :, C_GATE_SRC:].astype(BF16)
    uq = w_uq.reshape(depth, Q_RANK, MLA_HEADS, MLA_NOPE + MLA_ROPE)
    wq = jnp.concatenate([
        uq[..., :MLA_NOPE].reshape(depth, Q_RANK, -1),
        uq[..., MLA_NOPE:MLA_NOPE + ROPE_HALF].reshape(depth, Q_RANK, -1),
        uq[..., MLA_NOPE + ROPE_HALF:].reshape(depth, Q_RANK, -1)], axis=-1).astype(BF16)
    eye = jnp.eye(MLA_HEADS, dtype=w_uk.dtype)
    uk = jnp.transpose(w_uk, (0, 2, 3, 1))
    wuk = (eye[None, :, None, :, None] * uk[:, :, :, None, :]).reshape(
        depth, MLA_HEADS * MLA_NOPE, MLA_HEADS * KV_RANK).astype(BF16)
    uv = jnp.transpose(w_uv, (0, 2, 1, 3))
    wuv = (eye[None, :, None, :, None] * uv[:, :, :, None, :]).reshape(
        depth, MLA_HEADS * KV_RANK, MLA_HEADS * MLA_V).astype(BF16)
    return wa, wg, wq, wuk, wuv


def _rope_tables(pos):
    inv = ROPE_THETA ** (-jnp.arange(ROPE_HALF, dtype=F32) / ROPE_HALF)
    ang = pos.astype(F32)[:, None] * inv[None, :]
    reps = V7X_LANES // ROPE_HALF
    return jnp.tile(jnp.cos(ang), (1, reps)), jnp.tile(jnp.sin(ang), (1, reps))


def _heads_q(q_lat, q_pe):
    n = q_lat.shape[0]
    pe = q_pe.reshape(n, 2, MLA_HEADS, ROPE_HALF).transpose(0, 2, 1, 3).reshape(n, MLA_HEADS, MLA_ROPE)
    return jnp.concatenate([q_lat.reshape(n, MLA_HEADS, KV_RANK), pe], axis=-1).reshape(n * MLA_HEADS, D_QK)


def _tiles(n_rows, short_seqs):
    return dict(in_proj=min(512, n_rows), merge=min(512, n_rows),
                ffn=min(256 if short_seqs else 512, n_rows))


FFN_COL_BLOCK = 256


POOL_TILE = 512
HGRN_TILE = 256
HGRN_CHUNK = 32
HGRN_SHORT_TILE = 128
ATTN_TQ = 256
ATTN_TK = 512
SAMPLE_KEY_CHUNKS = 1


def _trunk(x_prompt, x_sample, cache_ckv, cache_kpe, state_pool, state_hgrn, state_conv, page_table,
           w_in, pool_w, pool_scale, hg_lower_bounds, hg_norm, q_norm, w_uq, kv_norm, w_uk, w_uv,
           w_br_pool, w_br_hg, w_br_mla, w_out, ln1_g, ln1_b, ffn_w_up, ffn_conv_w, ffn_conv_b,
           ffn_w_down, ln2_g, ln2_b):
    depth = w_in.shape[0]
    b, t, _ = x_prompt.shape
    sb, ts, _ = x_sample.shape
    past_len = page_table.shape[1] * cache_ckv.shape[2]
    alpha = (2 * depth) ** 0.25

    wa, wg, wq, wuk, wuv = _pack_weights(w_in, w_uq, w_uk, w_uv)
    bf = lambda a: a.astype(BF16)
    pool_wb, wp, wh, wm, wo, wup, wdn = map(bf, (pool_w, w_br_pool, w_br_hg, w_br_mla, w_out, ffn_w_up,
                                                  ffn_w_down))
    row2 = lambda a: a.reshape(depth, 1, -1)
    pool_scale, hg_norm, q_norm, kv_norm, ln1_g, ln1_b, ln2_g, ln2_b, ffn_conv_b = map(
        row2, (pool_scale, hg_norm, q_norm, kv_norm, ln1_g, ln1_b, ln2_g, ln2_b, ffn_conv_b))

    np_, ns_ = b * t, sb * ts
    tp, tsm = _tiles(np_, False), _tiles(ns_, True)
    cos_p, sin_p = _rope_tables(jnp.arange(t, dtype=jnp.int32))
    cos_s, sin_s = _rope_tables(past_len + jnp.arange(tsm["in_proj"], dtype=jnp.int32) % ts)

    xp = x_prompt.reshape(np_, D_MODEL)
    xs = x_sample.reshape(ns_, D_MODEL)
    cache_kpe_t = jnp.swapaxes(cache_kpe, 2, 3)
    ones_rows = jnp.zeros((b, VT_ROWS - KV_RANK, t), BF16).at[:, 0, :].set(1.0)
    zero_state = jnp.zeros((b, N_SEG, SEG, SEG), F32)
    no_conv_state = jnp.zeros((1, CONV_W - 1, 2 * D_FF), F32)
    outs_p = [[] for _ in range(5)]
    outs_s = [[] for _ in range(5)]

    for l in range(depth):
        layer = jnp.full((1,), l, jnp.int32)

        (u, qa, kk, lf, vv, sg, q_lat, q_pe, ckv, kpe, kcat) = _in_proj(
            layer, xp, wa[l], hg_lower_bounds, q_norm[l], kv_norm[l], wq[l], wuk[l], cos_p, sin_p,
            tp["in_proj"])
        y_pool = _pool_prompt(u.reshape(b, t, D_BR), pool_wb[l], pool_scale[l], min(POOL_TILE, t))
        o_hg, s_fin = _hgrn(qa, kk, lf, vv, sg, hg_norm[l], zero_state, b, t, min(HGRN_TILE, t),
                            min(HGRN_CHUNK, t))
        kcat = kcat.reshape(b, t, D_QK)
        vt = jnp.concatenate([jnp.swapaxes(kcat[:, :, :KV_RANK], 1, 2), ones_rows], axis=1)
        o_lat = _attn_prompt(q_lat, q_pe, kcat, vt, min(ATTN_TQ, t), min(ATTN_TK, t))
        x1 = _merge(xp, y_pool.reshape(np_, D_BR), o_hg, o_lat, wg[l],
                    wuv[l], wp[l], wh[l], wm[l], wo[l], ln1_g[l], ln1_b[l], alpha, tp["merge"])
        xp, tail = _ffn(x1, wup[l], ffn_conv_w[l], ffn_conv_b[l], wdn[l], ln2_g[l], ln2_b[l],
                        no_conv_state, alpha, tp["ffn"], t)
        tiles_per_seq = t // tp["ffn"]
        conv_p = tail.reshape(b, tiles_per_seq, V7X_SUBLANES, 2 * D_FF)[:, -1, -(CONV_W - 1):, :]
        for lst, v in zip(outs_p, (ckv.reshape(b, t, KV_RANK), kpe.reshape(b, t, MLA_ROPE),
                                   u.reshape(b, t, D_BR)[:, -POOL_BUF:], s_fin, conv_p)):
            lst.append(v)

        (u, qa, kk, lf, vv, sg, q_lat, q_pe, ckv, kpe, _) = _in_proj(
            layer, xs, wa[l], hg_lower_bounds, q_norm[l], kv_norm[l], wq[l], wuk[l], cos_s, sin_s,
            tsm["in_proj"])
        ext = jnp.concatenate([state_pool[l], u.reshape(sb, ts, D_BR)], axis=1)
        y_pool = _pool_sample(jnp.swapaxes(ext, 0, 1), pool_wb[l], pool_scale[l], ts, past_len)
        y_pool = jnp.swapaxes(y_pool, 0, 1).reshape(ns_, D_BR)
        o_hg, s_fin = _hgrn(qa, kk, lf, vv, sg, hg_norm[l], state_hgrn[l], sb, ts,
                            min(HGRN_SHORT_TILE, ns_), ts)
        k_new = jnp.concatenate([ckv, kpe], axis=-1).reshape(sb, ts, D_QK)
        o_lat = _attn_sample(layer, page_table, _heads_q(q_lat, q_pe).reshape(sb, ts * MLA_HEADS, D_QK),
                             k_new, ckv.reshape(sb, ts, KV_RANK), cache_ckv, cache_kpe_t)
        x1 = _merge(xs, y_pool, o_hg, o_lat.reshape(ns_, MLA_HEADS * KV_RANK), wg[l],
                    wuv[l], wp[l], wh[l], wm[l], wo[l], ln1_g[l], ln1_b[l], alpha, tsm["merge"])
        xs, up = _ffn(x1, wup[l], ffn_conv_w[l], ffn_conv_b[l], wdn[l], ln2_g[l], ln2_b[l],
                      state_conv[l], alpha, tsm["ffn"], ts)
        conv_s = jnp.concatenate([state_conv[l], up.reshape(sb, ts, 2 * D_FF)], axis=1)[:, -(CONV_W - 1):]
        for lst, v in zip(outs_s, (ckv.reshape(sb, ts, KV_RANK), kpe.reshape(sb, ts, MLA_ROPE),
                                   ext[:, -POOL_BUF:], s_fin, conv_s)):
            lst.append(v)

    stack = lambda lists: [jnp.stack(v) for v in lists]
    return (xp.reshape(b, t, D_MODEL), xs.reshape(sb, ts, D_MODEL), *stack(outs_p), *stack(outs_s))


def kernel(x_prompt, x_sample, cache_ckv, cache_kpe, state_pool, state_hgrn, state_conv, page_table, w_in, pool_w, pool_scale, hg_lower_bounds, hg_norm, q_norm, w_uq, kv_norm, w_uk, w_uv, w_br_pool, w_br_hg, w_br_mla, w_out, ln1_g, ln1_b, ffn_w_up, ffn_conv_w, ffn_conv_b, ffn_w_down, ln2_g, ln2_b):
    return _trunk(x_prompt, x_sample, cache_ckv, cache_kpe, state_pool, state_hgrn, state_conv, page_table,
                  w_in, pool_w, pool_scale, hg_lower_bounds, hg_norm, q_norm, w_uq, kv_norm, w_uk, w_uv,
                  w_br_pool, w_br_hg, w_br_mla, w_out, ln1_g, ln1_b, ffn_w_up, ffn_conv_w, ffn_conv_b,
                  ffn_w_down, ln2_g, ln2_b)
```

```python
import functools

import jax
import jax.numpy as jnp
from jax import lax
from jax.experimental import pallas as pl
from jax.experimental.pallas import tpu as pltpu

F32 = jnp.float32
BF16 = jnp.bfloat16

D_MODEL = 1024
POOL_WINDOWS = (2, 4, 8, 16)
SEG = 128
N_SEG = 4
D_BR = N_SEG * SEG
POOL_BUF = max(POOL_WINDOWS) - 1
MLA_HEADS = 8
MLA_NOPE = 64
MLA_ROPE = 32
ROPE_HALF = MLA_ROPE // 2
MLA_V = 64
Q_RANK = 256
KV_RANK = SEG
D_QK = KV_RANK + MLA_ROPE
VT_ROWS = KV_RANK + 16
D_FF = 2816
CONV_W = 3
MLA_SCALE = (MLA_NOPE + MLA_ROPE) ** -0.5
LOG2_E = 1.4426950408889634
Q_SCALE = MLA_SCALE * LOG2_E
ROPE_THETA = 10000.0
LN_EPS = 1e-5
RMS_EPS = 1e-6

V7X_VMEM_BYTES = 64 * 1024 * 1024
V7X_LANES = 128
V7X_SUBLANES = 8
VMEM_LIMIT = V7X_VMEM_BYTES - 6 * 1024 * 1024

C_POOL, C_HQ, C_HF, C_HI, C_HG, C_CQ, C_CKV, C_K1, C_K2, C_END = (
    0, 512, 1024, 1536, 2048, 2560, 2816, 2944, 3072, 3200)
C_KPE_SRC = 2944
C_GATE_SRC = C_KPE_SRC + MLA_ROPE

NEG_BIG = -1e30
HGRN_SAFE_LOG_DECAY = -60.0


def _cparams(sem):
    return pltpu.CompilerParams(dimension_semantics=sem, vmem_limit_bytes=VMEM_LIMIT)


def _const_spec(shape):
    nd = len(shape)
    return pl.BlockSpec(shape, lambda *_: (0,) * nd, pipeline_mode=pl.Buffered(1))


def _layer_spec(arr):
    rest = arr.shape[1:]
    return pl.BlockSpec((None,) + rest, lambda *a: (a[-1][0],) + (0,) * len(rest),
                        pipeline_mode=pl.Buffered(1))


def _layer_grid(grid, in_specs, out_specs, scratch_shapes=()):
    return pltpu.PrefetchScalarGridSpec(num_scalar_prefetch=1, grid=grid, in_specs=in_specs,
                                        out_specs=out_specs, scratch_shapes=list(scratch_shapes))


def _sigmoid(x):
    return 0.5 * (jnp.tanh(0.5 * x) + 1.0)


def _silu(x):
    half = 0.5 * x
    return half + half * jnp.tanh(half)


def _layer_norm(h, g, b):
    mu = jnp.mean(h, axis=-1, keepdims=True)
    d = h - mu
    var = jnp.mean(d * d, axis=-1, keepdims=True)
    return d * lax.rsqrt(var + LN_EPS) * g + b


def _in_proj_kernel(layer_ref, x_ref, w_ref, hglb_ref, qn_ref, kvn_ref, wq_ref, wuk_ref, cos_ref, sin_ref,
                    u_ref, qa_ref, kk_ref, lf_ref, vv_ref, sg_ref, qlat_ref, qpe_ref, ckv_ref, kpe_ref,
                    kcat_ref):
    xb = x_ref[...].astype(BF16)

    def seg(lo, hi):
        return jnp.dot(xb, w_ref[:, lo:hi], preferred_element_type=F32)

    u_ref[...] = seg(C_POOL, C_HQ)

    qa_ref[...] = _silu(seg(C_HQ, C_HF))

    layer = layer_ref[0]
    hb = hglb_ref[...]
    e = jnp.exp(hb - jnp.max(hb, axis=0, keepdims=True))
    row = lax.broadcasted_iota(jnp.int32, hb.shape, 0)
    lb = (jnp.sum(jnp.where((row >= 1) & (row <= layer), e, 0.0), axis=0, keepdims=True)
          / jnp.sum(e, axis=0, keepdims=True))

    fl = seg(C_HF, C_HI)
    log_sig = jnp.minimum(fl, 0.0) - jnp.log(1.0 + jnp.exp(-jnp.abs(fl)))
    a = jnp.log(lb)
    b = jnp.log1p(-lb) + log_sig
    lf_ref[...] = jnp.maximum(a, b) + jnp.log(1.0 + jnp.exp(-jnp.abs(a - b)))
    kk_ref[...] = (1.0 - lb) * _sigmoid(-fl)

    vv_ref[...] = seg(C_HI, C_HG)
    sg_ref[...] = _silu(seg(C_HG, C_CQ))

    cq = seg(C_CQ, C_CKV)
    cqn = cq * lax.rsqrt(jnp.mean(cq * cq, axis=-1, keepdims=True) + RMS_EPS) * qn_ref[...]
    q = jnp.dot(cqn.astype(BF16), wq_ref[...], preferred_element_type=F32)
    cos = cos_ref[...]
    sin = sin_ref[...]
    n_nope = MLA_HEADS * MLA_NOPE
    x1 = q[:, n_nope:n_nope + V7X_LANES]
    x2 = q[:, n_nope + V7X_LANES:]
    qpe_ref[:, :V7X_LANES] = ((x1 * cos - x2 * sin) * Q_SCALE).astype(BF16)
    qpe_ref[:, V7X_LANES:] = ((x1 * sin + x2 * cos) * Q_SCALE).astype(BF16)
    qlat = jnp.dot(q[:, :n_nope].astype(BF16), wuk_ref[...], preferred_element_type=F32)
    qlat_ref[...] = (qlat * Q_SCALE).astype(BF16)

    ckv = seg(C_CKV, C_K1)
    ckv = ckv * lax.rsqrt(jnp.mean(ckv * ckv, axis=-1, keepdims=True) + RMS_EPS) * kvn_ref[...]
    ckv_ref[...] = ckv
    kcat_ref[:, :KV_RANK] = ckv.astype(BF16)

    k1 = seg(C_K1, C_K2)
    k2 = seg(C_K2, C_END)
    lane = lax.broadcasted_iota(jnp.int32, sin.shape, 1)
    kr = k1 * cos + k2 * jnp.where(lane < ROPE_HALF, -sin, sin)
    kpe_ref[...] = kr[:, :MLA_ROPE]
    kcat_ref[:, KV_RANK:] = kr[:, :MLA_ROPE].astype(BF16)


def _in_proj(layer, x, wa, hglb, q_norm, kv_norm, wq, wuk, cos_tab, sin_tab, tm):
    n = x.shape[0]
    tab_blocks = cos_tab.shape[0] // tm
    row = lambda w: pl.BlockSpec((tm, w), lambda i, *_: (i, 0))
    tab = pl.BlockSpec((tm, V7X_LANES), lambda i, *_: (i % tab_blocks, 0))
    widths = (D_BR,) * 6 + (MLA_HEADS * KV_RANK, MLA_HEADS * MLA_ROPE, KV_RANK, MLA_ROPE, D_QK)
    dtypes = (F32,) * 6 + (BF16, BF16, F32, F32, BF16)
    grid_spec = _layer_grid(
        (n // tm,),
        [row(D_MODEL), _layer_spec(wa), _const_spec(hglb.shape), _layer_spec(q_norm), _layer_spec(kv_norm),
         _layer_spec(wq), _layer_spec(wuk), tab, tab],
        [row(w) for w in widths])
    return pl.pallas_call(
        _in_proj_kernel,
        grid_spec=grid_spec,
        out_shape=[jax.ShapeDtypeStruct((n, w), d) for w, d in zip(widths, dtypes)],
        compiler_params=_cparams(("arbitrary",)),
        name="in_proj",
    )(layer, x, wa, hglb, q_norm, kv_norm, wq, wuk, cos_tab, sin_tab)


def _pool_prompt_kernel(layer_ref, u_ref, w_ref, sc_ref, y_ref, ext_ref, *, tt):
    del layer_ref
    halo = 2 * V7X_SUBLANES
    j = pl.program_id(1)

    @pl.when(j == 0)
    def _():
        ext_ref[0:halo, :] = jnp.zeros((halo, D_BR), F32)

    u = u_ref[0]
    ext_ref[halo:halo + tt, :] = u
    pos = j * tt + lax.broadcasted_iota(jnp.int32, (tt, 1), 0)
    for g, win in enumerate(POOL_WINDOWS):
        c0 = g * SEG
        acc = u[:, c0:c0 + SEG]
        for d in range(1, win):
            acc = acc + ext_ref[halo - d:halo - d + tt, c0:c0 + SEG]
        cnt = jnp.minimum(pos + 1, win).astype(F32)
        diff = (acc / cnt - u[:, c0:c0 + SEG]).astype(BF16)
        y = jnp.dot(diff, w_ref[g], preferred_element_type=F32) * sc_ref[:, c0:c0 + SEG]
        y_ref[0, :, c0:c0 + SEG] = y.astype(BF16)
    ext_ref[0:halo, :] = ext_ref[tt:tt + halo, :]


def _pool_prompt(layer, u, w_pool, scale, tt):
    b, t, _ = u.shape
    blk = pl.BlockSpec((1, tt, D_BR), lambda i, j, *_: (i, j, 0))
    return pl.pallas_call(
        functools.partial(_pool_prompt_kernel, tt=tt),
        grid_spec=_layer_grid((b, t // tt), [blk, _layer_spec(w_pool), _layer_spec(scale)], blk,
                              [pltpu.VMEM((tt + 2 * V7X_SUBLANES, D_BR), F32)]),
        out_shape=jax.ShapeDtypeStruct((b, t, D_BR), BF16),
        compiler_params=_cparams(("arbitrary", "arbitrary")),
        name="pool_prompt",
    )(layer, u, w_pool, scale)


def _pool_sample_kernel(layer_ref, ext_ref, w_ref, sc_ref, y_ref, *, t_new, past_len):
    del layer_ref
    s = ext_ref.shape[1]
    for g, win in enumerate(POOL_WINDOWS):
        c0 = g * SEG
        diffs = []
        for t in range(t_new):
            cur = ext_ref[POOL_BUF + t, :, c0:c0 + SEG]
            acc = cur
            for d in range(1, win):
                acc = acc + ext_ref[POOL_BUF + t - d, :, c0:c0 + SEG]
            cnt = float(min(past_len + t + 1, win))
            diffs.append(acc / cnt - cur)
        diff = jnp.concatenate(diffs, axis=0).astype(BF16)
        y = jnp.dot(diff, w_ref[g], preferred_element_type=F32) * sc_ref[:, c0:c0 + SEG]
        y_ref[:, :, c0:c0 + SEG] = y.reshape(t_new, s, SEG).astype(BF16)


def _pool_sample(layer, ext_tm, w_pool, scale, t_new, past_len):
    _, s, _ = ext_tm.shape
    return pl.pallas_call(
        functools.partial(_pool_sample_kernel, t_new=t_new, past_len=past_len),
        grid_spec=_layer_grid((1,), [_const_spec(ext_tm.shape), _layer_spec(w_pool), _layer_spec(scale)],
                              pl.BlockSpec((t_new, s, D_BR), lambda i, *_: (0, 0, 0))),
        out_shape=jax.ShapeDtypeStruct((t_new, s, D_BR), BF16),
        compiler_params=_cparams(("arbitrary",)),
        name="pool_sample",
    )(layer, ext_tm, w_pool, scale)


def _hgrn_kernel(layer_ref, qa_ref, kk_ref, lf_ref, vv_ref, sg_ref, nw_ref, *rest, c, n_chunks, chained):
    del layer_ref
    if chained:
        o_ref, sfin_ref, st_ref, lsc_ref, ksc_ref, ot_ref = rest
    else:
        s0_ref, o_ref, sfin_ref, st_ref, lsc_ref, ksc_ref, ot_ref = rest
    tile = c * n_chunks
    log2c = c.bit_length() - 1
    j = pl.program_id(1)

    if chained:
        @pl.when(j == 0)
        def _():
            st_ref[...] = jnp.zeros(st_ref.shape, F32)

    r_i = lax.broadcasted_iota(jnp.int32, (tile, tile), 0)
    c_i = lax.broadcasted_iota(jnp.int32, (tile, tile), 1)
    chunk_of_row = jnp.right_shift(r_i, log2c)
    blk_causal = (chunk_of_row == jnp.right_shift(c_i, log2c)) & (r_i >= c_i)
    tri = blk_causal.astype(BF16)

    chunk_of_row_seg = jnp.right_shift(lax.broadcasted_iota(jnp.int32, (tile, SEG), 0), log2c)
    chunk_of_col = jnp.right_shift(lax.broadcasted_iota(jnp.int32, (SEG, tile), 1), log2c)

    def nt(a, b):
        return lax.dot_general(a, b, (((1,), (1,)), ((), ())), preferred_element_type=F32)

    def per_chunk(x3):
        return jnp.broadcast_to(x3, (n_chunks, c, D_BR)).reshape(tile, D_BR)

    lf = lf_ref[...]
    hi = lf.astype(BF16)
    r1 = lf - hi.astype(F32)
    mid = r1.astype(BF16)
    lo = (r1 - mid.astype(F32)).astype(BF16)
    cum = (jnp.dot(tri, hi, preferred_element_type=F32) + jnp.dot(tri, mid, preferred_element_type=F32)
           + jnp.dot(tri, lo, preferred_element_type=F32))
    cum3 = cum.reshape(n_chunks, c, D_BR)
    tot = per_chunk(cum3[:, c - 1:c, :])
    qa = qa_ref[...]
    kk = kk_ref[...]
    vv = vv_ref[...]
    q_in = qa * jnp.exp(cum)
    k_out = kk * jnp.exp(tot - cum)
    dec = jnp.exp(tot)
    safe = jnp.min(cum) >= HGRN_SAFE_LOG_DECAY

    def fast():
        k_rel = kk * jnp.exp(-cum)
        return [nt(q_in[:, h * SEG:(h + 1) * SEG].astype(BF16), k_rel[:, h * SEG:(h + 1) * SEG].astype(BF16))
                for h in range(N_SEG)]

    def exact():
        lsc_ref[...] = cum3
        ksc_ref[...] = kk.reshape(n_chunks, c, D_BR)

        def col(s, accs):
            w = jnp.exp(jnp.minimum(cum - per_chunk(lsc_ref[:, pl.ds(s, 1), :]), 0.0))
            prod = qa * per_chunk(ksc_ref[:, pl.ds(s, 1), :]) * w
            here = c_i == chunk_of_row * c + s
            return [jnp.where(here, jnp.sum(prod[:, h * SEG:(h + 1) * SEG], axis=-1, keepdims=True), accs[h])
                    for h in range(N_SEG)]

        return lax.fori_loop(0, c, col, [jnp.zeros((tile, tile), F32)] * N_SEG)

    a_all = lax.cond(safe, fast, exact)

    for h in range(N_SEG):
        cols = slice(h * SEG, (h + 1) * SEG)
        a_h = jnp.where(blk_causal, a_all[h], 0.0).astype(BF16)
        o_intra = jnp.dot(a_h, vv[:, cols].astype(BF16), preferred_element_type=F32)
        k_h = k_out[:, cols]
        kexp = jnp.concatenate([jnp.where(chunk_of_row_seg == ci, k_h, 0.0) for ci in range(n_chunks)],
                               axis=1).astype(BF16)
        kv_all = jnp.dot(vv[:, cols].T.astype(BF16), kexp, preferred_element_type=F32)
        states = []
        st = st_ref[h] if chained else None
        for ci in range(n_chunks):
            if not chained:
                st = s0_ref[ci, h].T
            states.append(st)
            st = st * dec[ci * c:ci * c + 1, cols] + kv_all[:, ci * SEG:(ci + 1) * SEG]
            if not chained:
                sfin_ref[ci, h] = st.T
        if chained:
            st_ref[h] = st
        q_t = q_in[:, cols].T
        qexp_t = jnp.concatenate([jnp.where(chunk_of_col == ci, q_t, 0.0) for ci in range(n_chunks)],
                                 axis=0).astype(BF16)
        ot_ref[...] = jnp.dot(jnp.concatenate(states, axis=1).astype(BF16), qexp_t,
                              preferred_element_type=F32)
        o = o_intra + ot_ref[...].T
        on = o * lax.rsqrt(jnp.mean(o * o, axis=-1, keepdims=True) + RMS_EPS) * nw_ref[:, cols]
        o_ref[:, cols] = (on * sg_ref[:, cols]).astype(BF16)

    if chained:
        @pl.when(j == pl.num_programs(1) - 1)
        def _():
            for h in range(N_SEG):
                sfin_ref[0, h] = st_ref[h].T


def _hgrn(layer, qa, kk, lf, vv, sg, norm_w, states, n_seq, t_len, tile, c):
    n = qa.shape[0]
    assert c & (c - 1) == 0 and tile % c == 0
    n_chunks = tile // c
    chained = states is None
    if chained:
        steps = t_len // tile
        grid = (n_seq, steps)
        row = pl.BlockSpec((tile, D_BR), lambda i, j, *_: (i * steps + j, 0))
        out_st = pl.BlockSpec((1, N_SEG, SEG, SEG), lambda i, j, *_: (i, 0, 0, 0))
        st_in = []
    else:
        assert t_len == c and n_seq % n_chunks == 0
        grid = (n_seq // n_chunks, 1)
        row = pl.BlockSpec((tile, D_BR), lambda i, j, *_: (i, 0))
        out_st = pl.BlockSpec((n_chunks, N_SEG, SEG, SEG), lambda i, j, *_: (i, 0, 0, 0))
        st_in = [pl.BlockSpec((None, n_chunks, N_SEG, SEG, SEG), lambda i, j, lyr: (lyr[0], i, 0, 0, 0))]
    return pl.pallas_call(
        functools.partial(_hgrn_kernel, c=c, n_chunks=n_chunks, chained=chained),
        grid_spec=_layer_grid(grid, [row] * 5 + [_layer_spec(norm_w)] + st_in, [row, out_st],
                              [pltpu.VMEM((N_SEG, SEG, SEG), F32), pltpu.VMEM((n_chunks, c, D_BR), F32),
                               pltpu.VMEM((n_chunks, c, D_BR), F32), pltpu.VMEM((SEG, tile), F32)]),
        out_shape=[jax.ShapeDtypeStruct((n, D_BR), BF16),
                   jax.ShapeDtypeStruct((n_seq, N_SEG, SEG, SEG), F32)],
        compiler_params=_cparams(("arbitrary", "arbitrary")),
        name="hgrn",
    )(layer, qa, kk, lf, vv, sg, norm_w, *([] if chained else [states]))


def _attn_prompt_kernel(ql_ref, qp_ref, k_ref, vt_ref, o_ref, qt_ref, m_ref, acc_ref, *, tq, tk):
    qi = pl.program_id(1)
    cols = tq * MLA_HEADS

    qpt = qp_ref[...].astype(F32).T
    for h in range(MLA_HEADS):
        cs = slice(h * tq, (h + 1) * tq)
        qt_ref[0:KV_RANK, cs] = ql_ref[:, h * KV_RANK:(h + 1) * KV_RANK].astype(F32).T.astype(BF16)
        qt_ref[KV_RANK:KV_RANK + ROPE_HALF, cs] = qpt[h * ROPE_HALF:(h + 1) * ROPE_HALF].astype(BF16)
        qt_ref[KV_RANK + ROPE_HALF:D_QK, cs] = qpt[V7X_LANES + h * ROPE_HALF:
                                                  V7X_LANES + (h + 1) * ROPE_HALF].astype(BF16)

    m_ref[...] = jnp.full((1, cols), NEG_BIG, F32)
    acc_ref[...] = jnp.zeros(acc_ref.shape, F32)

    grp = min(ATTN_COL_GROUP, cols)

    def step(ks, masked):
        kb = k_ref[0, pl.ds(ks, tk), :]
        vtb = vt_ref[0, :, pl.ds(ks, tk)]
        if masked:
            key = ks + lax.broadcasted_iota(jnp.int32, (tk, grp), 0)
            tok = qi * tq + (lax.broadcasted_iota(jnp.int32, (tk, grp), 1) & (tq - 1))
            visible = key <= tok
        for g in range(cols // grp):
            cs = slice(g * grp, (g + 1) * grp)
            s = jnp.dot(kb, qt_ref[:, cs], preferred_element_type=F32)
            if masked:
                s = jnp.where(visible, s, NEG_BIG)
            m_old = m_ref[:, cs]
            m_new = jnp.maximum(m_old, jnp.max(s, axis=0, keepdims=True))
            p = jnp.exp2(s - m_new).astype(BF16)
            acc_ref[:, cs] = jnp.exp2(m_old - m_new) * acc_ref[:, cs] + jnp.dot(
                vtb, p, preferred_element_type=F32)
            m_ref[:, cs] = m_new

    n_full = (qi * tq) // tk

    def body(jb, carry):
        step(pl.multiple_of(jb * tk, tk), False)
        return carry

    lax.fori_loop(0, n_full, body, 0)
    step(pl.multiple_of(n_full * tk, tk), True)
    o = acc_ref[0:KV_RANK, :] / acc_ref[KV_RANK:KV_RANK + 1, :]
    for h in range(MLA_HEADS):
        o_ref[:, h * KV_RANK:(h + 1) * KV_RANK] = o[:, h * tq:(h + 1) * tq].T.astype(BF16)


def _attn_prompt(q_lat, q_pe, k, vt, tq, tk):
    b, t, _ = k.shape
    assert t % tk == 0 and tk % tq == 0 and tq & (tq - 1) == 0
    nq = t // tq
    cols = tq * MLA_HEADS
    qrow = lambda w: pl.BlockSpec((tq, w), lambda i, j: (i * nq + j, 0))
    return pl.pallas_call(
        functools.partial(_attn_prompt_kernel, tq=tq, tk=tk),
        grid=(b, nq),
        in_specs=[qrow(MLA_HEADS * KV_RANK), qrow(MLA_HEADS * MLA_ROPE),
                  pl.BlockSpec((1, t, D_QK), lambda i, j: (i, 0, 0)),
                  pl.BlockSpec((1, VT_ROWS, t), lambda i, j: (i, 0, 0))],
        out_specs=qrow(MLA_HEADS * KV_RANK),
        out_shape=jax.ShapeDtypeStruct((b * t, MLA_HEADS * KV_RANK), BF16),
        scratch_shapes=[pltpu.VMEM((D_QK, cols), BF16), pltpu.VMEM((1, cols), F32),
                        pltpu.VMEM((VT_ROWS, cols), F32)],
        compiler_params=_cparams(("arbitrary", "arbitrary")),
        name="attn_prompt",
    )(q_lat, q_pe, k, vt)


def _attn_sample_kernel(layer_ref, pt_ref, q_ref, kn_ref, vn_ref, ckv_hbm, kpe_hbm, o_ref,
                        cbuf, pbuf, sem, *, n_pages, page, t_new):
    b = pl.program_id(0)
    nb = pl.num_programs(0)
    slot = b % 2
    layer = layer_ref[0]

    def page_copies(seq, slot_, p):
        pg = pt_ref[seq, p]
        dst = pl.ds(p * page, page)
        return (pltpu.make_async_copy(ckv_hbm.at[layer, pg], cbuf.at[slot_, dst, :], sem.at[0, slot_]),
                pltpu.make_async_copy(kpe_hbm.at[layer, pg], pbuf.at[slot_, :, dst], sem.at[1, slot_]))

    def start_all(seq, slot_):
        for p in range(n_pages):
            for cp in page_copies(seq, slot_, p):
                cp.start()

    @pl.when(b == 0)
    def _():
        start_all(0, 0)

    @pl.when(b + 1 < nb)
    def _():
        start_all(b + 1, 1 - slot)

    for p in range(n_pages):
        for cp in page_copies(b, slot, p):
            cp.wait()

    q = q_ref[0]
    nt = (((1,), (1,)), ((), ()))
    n_ch = SAMPLE_KEY_CHUNKS if n_pages % SAMPLE_KEY_CHUNKS == 0 else 1
    ck = n_pages * page // n_ch
    parts = []
    for i in range(n_ch):
        kc = cbuf[slot, i * ck:(i + 1) * ck, :].astype(BF16)
        s = (lax.dot_general(q[:, :KV_RANK], kc, nt, preferred_element_type=F32)
             + jnp.dot(q[:, KV_RANK:], pbuf[slot, :, i * ck:(i + 1) * ck].astype(BF16),
                       preferred_element_type=F32))
        m_i = jnp.max(s, axis=-1, keepdims=True)
        p = jnp.exp2(s - m_i)
        parts.append((m_i, jnp.sum(p, axis=-1, keepdims=True),
                      jnp.dot(p.astype(BF16), kc, preferred_element_type=F32)))
    s_new = lax.dot_general(q.astype(F32), kn_ref[0], nt, preferred_element_type=F32)
    rows = t_new * MLA_HEADS
    tok = lax.broadcasted_iota(jnp.int32, (rows, t_new), 0) // MLA_HEADS
    key = lax.broadcasted_iota(jnp.int32, (rows, t_new), 1)
    s_new = jnp.where(key <= tok, s_new, NEG_BIG)
    m_n = jnp.max(s_new, axis=-1, keepdims=True)
    p_new = jnp.exp2(s_new - m_n)
    parts.append((m_n, jnp.sum(p_new, axis=-1, keepdims=True),
                  jnp.dot(p_new, vn_ref[0], preferred_element_type=F32)))
    m = functools.reduce(jnp.maximum, [pt[0] for pt in parts])
    w = [jnp.exp2(pt[0] - m) for pt in parts]
    denom = sum(wi * pt[1] for wi, pt in zip(w, parts))
    o = sum(wi * pt[2] for wi, pt in zip(w, parts))
    o_ref[0] = (o / denom).astype(BF16)


def _attn_sample(layer, page_table, q, k_new, v_new, cache_ckv, cache_kpe_t):
    n_seq, n_pages = page_table.shape
    page = cache_ckv.shape[2]
    t_new = k_new.shape[1]
    rows = t_new * MLA_HEADS
    past = n_pages * page
    grid_spec = pltpu.PrefetchScalarGridSpec(
        num_scalar_prefetch=2,
        grid=(n_seq,),
        in_specs=[pl.BlockSpec((1, rows, D_QK), lambda i, *_: (i, 0, 0)),
                  pl.BlockSpec((1, t_new, D_QK), lambda i, *_: (i, 0, 0)),
                  pl.BlockSpec((1, t_new, KV_RANK), lambda i, *_: (i, 0, 0)),
                  pl.BlockSpec(memory_space=pl.ANY), pl.BlockSpec(memory_space=pl.ANY)],
        out_specs=pl.BlockSpec((1, rows, KV_RANK), lambda i, *_: (i, 0, 0)),
        scratch_shapes=[pltpu.VMEM((2, past, KV_RANK), F32), pltpu.VMEM((2, MLA_ROPE, past), F32),
                        pltpu.SemaphoreType.DMA((2, 2))],
    )
    return pl.pallas_call(
        functools.partial(_attn_sample_kernel, n_pages=n_pages, page=page, t_new=t_new),
        grid_spec=grid_spec,
        out_shape=jax.ShapeDtypeStruct((n_seq, rows, KV_RANK), BF16),
        compiler_params=_cparams(("arbitrary",)),
        name="attn_sample",
    )(layer, page_table, q, k_new, v_new, cache_ckv, cache_kpe_t)


def _merge_kernel(layer_ref, x_ref, yp_ref, oh_ref, ol_ref, wg_ref, wuv_ref, wp_ref, wh_ref, wm_ref, wo_ref,
                  g_ref, b_ref, y_ref, *, alpha):
    del layer_ref
    dot = functools.partial(jnp.dot, preferred_element_type=F32)
    x = x_ref[...]
    xb = x.astype(BF16)

    def gate(i):
        return _sigmoid(dot(xb, wg_ref[:, i * D_MODEL:(i + 1) * D_MODEL]))

    o_mla = dot(ol_ref[...], wuv_ref[...]).astype(BF16)
    merged = (gate(0) * dot(yp_ref[...], wp_ref[...]) + gate(1) * dot(oh_ref[...], wh_ref[...])
              + gate(2) * dot(o_mla, wm_ref[...]))
    h = alpha * x + dot(merged.astype(BF16), wo_ref[...])
    y_ref[...] = _layer_norm(h, g_ref[...], b_ref[...])


def _merge(layer, x, y_pool, o_hg, o_lat, wg, wuv, wp, wh, wm, wo, g, b, alpha, tm):
    n = x.shape[0]
    row = lambda w: pl.BlockSpec((tm, w), lambda i, *_: (i, 0))
    consts = (wg, wuv, wp, wh, wm, wo, g, b)
    return pl.pallas_call(
        functools.partial(_merge_kernel, alpha=alpha),
        grid_spec=_layer_grid((n // tm,),
                              [row(D_MODEL), row(D_BR), row(D_BR), row(MLA_HEADS * KV_RANK)]
                              + [_layer_spec(a) for a in consts], row(D_MODEL)),
        out_shape=jax.ShapeDtypeStruct((n, D_MODEL), F32),
        compiler_params=_cparams(("arbitrary",)),
        name="merge",
    )(layer, x, y_pool, o_hg, o_lat, *consts)


def _ffn_kernel(layer_ref, x_ref, wup_ref, cw_ref, cb_ref, wdn_ref, g_ref, b_ref, *rest, alpha, tm, seq_rows):
    del layer_ref
    long_seq = seq_rows % tm == 0
    if long_seq:
        y_ref, tail_ref, halo_ref, act_ref = rest
    else:
        st_ref, y_ref, tail_ref, halo_ref, act_ref = rest
    x = x_ref[...]
    xb = x.astype(BF16)
    sub = V7X_SUBLANES
    if long_seq:
        @pl.when(pl.program_id(0) % (seq_rows // tm) == 0)
        def _():
            halo_ref[...] = jnp.zeros(halo_ref.shape, F32)

        r8 = lax.broadcasted_iota(jnp.int32, (sub, 1), 0)
    else:
        n_s = tm // seq_rows
        t = lax.broadcasted_iota(jnp.int32, (tm, 1), 0) % seq_rows

    def conv_cols(lo):
        cs = slice(lo, lo + FFN_COL_BLOCK)
        up = jnp.dot(xb, wup_ref[:, cs], preferred_element_type=F32)
        prev1 = pltpu.roll(up, 1, 0)
        prev2 = pltpu.roll(up, 2, 0)
        if long_seq:
            last = halo_ref[sub - 1:sub, cs]
            last2 = halo_ref[sub - 2:sub - 1, cs]
            head1 = jnp.where(r8 == 0, last, prev1[:sub])
            head2 = jnp.where(r8 == 0, last2, jnp.where(r8 == 1, last, prev2[:sub]))
            prev1 = jnp.concatenate([head1, prev1[sub:]], axis=0)
            prev2 = jnp.concatenate([head2, prev2[sub:]], axis=0)
            halo_ref[:, cs] = up[tm - sub:, :]
            tail_ref[0, :, cs] = up[tm - sub:, :]
        else:
            st = st_ref[:, :, cs]
            wide = (n_s, seq_rows, FFN_COL_BLOCK)
            last = jnp.broadcast_to(st[:, 1:2, :], wide).reshape(tm, FFN_COL_BLOCK)
            last2 = jnp.broadcast_to(st[:, 0:1, :], wide).reshape(tm, FFN_COL_BLOCK)
            prev1 = jnp.where(t == 0, last, prev1)
            prev2 = jnp.where(t == 0, last2, jnp.where(t == 1, last, prev2))
            tail_ref[:, cs] = up
        return cb_ref[:, cs] + prev2 * cw_ref[0:1, cs] + prev1 * cw_ref[1:2, cs] + up * cw_ref[2:3, cs]

    for jb in range(D_FF // FFN_COL_BLOCK):
        lo = jb * FFN_COL_BLOCK
        act_ref[:, lo:lo + FFN_COL_BLOCK] = (_silu(conv_cols(lo)) * conv_cols(D_FF + lo)).astype(BF16)
    f = jnp.dot(act_ref[...], wdn_ref[...], preferred_element_type=F32)
    y_ref[...] = _layer_norm(alpha * x + f, g_ref[...], b_ref[...])


def _ffn(layer, x, wup, cw, cb, wdn, g, b, states, alpha, tm, seq_rows):
    n = x.shape[0]
    row = pl.BlockSpec((tm, D_MODEL), lambda i, *_: (i, 0))
    long_seq = seq_rows % tm == 0
    assert long_seq == (states is None)
    if long_seq:
        st_in = []
        tail_spec = pl.BlockSpec((1, V7X_SUBLANES, 2 * D_FF), lambda i, *_: (i, 0, 0))
        tail_shape = jax.ShapeDtypeStruct((n // tm, V7X_SUBLANES, 2 * D_FF), F32)
    else:
        n_s = tm // seq_rows
        st_in = [pl.BlockSpec((None, n_s, CONV_W - 1, 2 * D_FF), lambda i, lyr: (lyr[0], i, 0, 0))]
        tail_spec = pl.BlockSpec((tm, 2 * D_FF), lambda i, *_: (i, 0))
        tail_shape = jax.ShapeDtypeStruct((n, 2 * D_FF), F32)
    consts = (wup, cw, cb, wdn, g, b)
    return pl.pallas_call(
        functools.partial(_ffn_kernel, alpha=alpha, tm=tm, seq_rows=seq_rows),
        grid_spec=_layer_grid((n // tm,), [row] + [_layer_spec(a) for a in consts] + st_in,
                              [row, tail_spec],
                              [pltpu.VMEM((V7X_SUBLANES, 2 * D_FF), F32), pltpu.VMEM((tm, D_FF), BF16)]),
        out_shape=[jax.ShapeDtypeStruct((n, D_MODEL), F32), tail_shape],
        compiler_params=_cparams(("arbitrary",)),
        name="ffn",
    )(layer, x, *consts, *([] if long_seq else [states]))


def _pack_weights(w_in, w_uq, w_uk, w_uv):
    depth = w_in.shape[0]
    x1 = w_in[:, :, C_KPE_SRC:C_KPE_SRC + ROPE_HALF]
    x2 = w_in[:, :, C_KPE_SRC + ROPE_HALF:C_GATE_SRC]
    pad = jnp.zeros((depth, D_MODEL, V7X_LANES - MLA_ROPE), w_in.dtype)
    wa = jnp.concatenate([w_in[:, :, :C_KPE_SRC], x1, x2, pad, x2, x1, pad], axis=-1).astype(BF16)
    wg = w_in[:, :, C_GATE_SRC:].astype(BF16)
    uq = w_uq.reshape(depth, Q_RANK, MLA_HEADS, MLA_NOPE + MLA_ROPE)
    wq = jnp.concatenate([
        uq[..., :MLA_NOPE].reshape(depth, Q_RANK, -1),
        uq[..., MLA_NOPE:MLA_NOPE + ROPE_HALF].reshape(depth, Q_RANK, -1),
        uq[..., MLA_NOPE + ROPE_HALF:].reshape(depth, Q_RANK, -1)], axis=-1).astype(BF16)
    eye = jnp.eye(MLA_HEADS, dtype=w_uk.dtype)
    uk = jnp.transpose(w_uk, (0, 2, 3, 1))
    wuk = (eye[None, :, None, :, None] * uk[:, :, :, None, :]).reshape(
        depth, MLA_HEADS * MLA_NOPE, MLA_HEADS * KV_RANK).astype(BF16)
    uv = jnp.transpose(w_uv, (0, 2, 1, 3))
    wuv = (eye[None, :, None, :, None] * uv[:, :, :, None, :]).reshape(
        depth, MLA_HEADS * KV_RANK, MLA_HEADS * MLA_V).astype(BF16)
    return wa, wg, wq, wuk, wuv


def _rope_tables(pos):
    inv = ROPE_THETA ** (-jnp.arange(ROPE_HALF, dtype=F32) / ROPE_HALF)
    ang = pos.astype(F32)[:, None] * inv[None, :]
    reps = V7X_LANES // ROPE_HALF
    return jnp.tile(jnp.cos(ang), (1, reps)), jnp.tile(jnp.sin(ang), (1, reps))


def _heads_q(q_lat, q_pe):
    n = q_lat.shape[0]
    pe = q_pe.reshape(n, 2, MLA_HEADS, ROPE_HALF).transpose(0, 2, 1, 3).reshape(n, MLA_HEADS, MLA_ROPE)
    return jnp.concatenate([q_lat.reshape(n, MLA_HEADS, KV_RANK), pe], axis=-1).reshape(n * MLA_HEADS, D_QK)


def _tiles(n_rows, short_seqs):
    return dict(in_proj=min(512, n_rows), merge=min(512, n_rows),
                ffn=min(256 if short_seqs else 512, n_rows))


FFN_COL_BLOCK = 256


POOL_TILE = 512
HGRN_TILE = 256
HGRN_CHUNK = 32
HGRN_SHORT_TILE = 128
ATTN_TQ = 256
ATTN_TK = 512
ATTN_COL_GROUP = 2048
SAMPLE_KEY_CHUNKS = 1


def _trunk(x_prompt, x_sample, cache_ckv, cache_kpe, state_pool, state_hgrn, state_conv, page_table,
           w_in, pool_w, pool_scale, hg_lower_bounds, hg_norm, q_norm, w_uq, kv_norm, w_uk, w_uv,
           w_br_pool, w_br_hg, w_br_mla, w_out, ln1_g, ln1_b, ffn_w_up, ffn_conv_w, ffn_conv_b,
           ffn_w_down, ln2_g, ln2_b):
    depth = w_in.shape[0]
    b, t, _ = x_prompt.shape
    sb, ts, _ = x_sample.shape
    past_len = page_table.shape[1] * cache_ckv.shape[2]
    alpha = (2 * depth) ** 0.25

    wa, wg, wq, wuk, wuv = _pack_weights(w_in, w_uq, w_uk, w_uv)
    bf = lambda a: a.astype(BF16)
    pool_wb, wp, wh, wm, wo, wup, wdn = map(bf, (pool_w, w_br_pool, w_br_hg, w_br_mla, w_out, ffn_w_up,
                                                  ffn_w_down))
    row2 = lambda a: a.reshape(depth, 1, -1)
    pool_scale, hg_norm, q_norm, kv_norm, ln1_g, ln1_b, ln2_g, ln2_b, ffn_conv_b = map(
        row2, (pool_scale, hg_norm, q_norm, kv_norm, ln1_g, ln1_b, ln2_g, ln2_b, ffn_conv_b))

    np_, ns_ = b * t, sb * ts
    tp, tsm = _tiles(np_, False), _tiles(ns_, True)
    cos_p, sin_p = _rope_tables(jnp.arange(t, dtype=jnp.int32))
    cos_s, sin_s = _rope_tables(past_len + jnp.arange(tsm["in_proj"], dtype=jnp.int32) % ts)

    xp = x_prompt.reshape(np_, D_MODEL)
    xs = x_sample.reshape(ns_, D_MODEL)
    cache_kpe_t = jnp.swapaxes(cache_kpe, 2, 3)
    ones_rows = jnp.zeros((b, VT_ROWS - KV_RANK, t), BF16).at[:, 0, :].set(1.0)
    outs_p = [[] for _ in range(5)]
    outs_s = [[] for _ in range(5)]
    proj_w = (wa, hg_lower_bounds, q_norm, kv_norm, wq, wuk)
    merge_w = (wg, wuv, wp, wh, wm, wo, ln1_g, ln1_b)
    ffn_w = (wup, ffn_conv_w, ffn_conv_b, wdn, ln2_g, ln2_b)

    for l in range(depth):
        layer = jnp.full((1,), l, jnp.int32)

        (u, qa, kk, lf, vv, sg, q_lat, q_pe, ckv, kpe, kcat) = _in_proj(
            layer, xp, *proj_w, cos_p, sin_p, tp["in_proj"])
        y_pool = _pool_prompt(layer, u.reshape(b, t, D_BR), pool_wb, pool_scale, min(POOL_TILE, t))
        o_hg, s_fin = _hgrn(layer, qa, kk, lf, vv, sg, hg_norm, None, b, t, min(HGRN_TILE, t),
                            min(HGRN_CHUNK, t))
        kcat = kcat.reshape(b, t, D_QK)
        vt = jnp.concatenate([jnp.swapaxes(kcat[:, :, :KV_RANK], 1, 2), ones_rows], axis=1)
        o_lat = _attn_prompt(q_lat, q_pe, kcat, vt, min(ATTN_TQ, t), min(ATTN_TK, t))
        x1 = _merge(layer, xp, y_pool.reshape(np_, D_BR), o_hg, o_lat, *merge_w, alpha, tp["merge"])
        xp, tail = _ffn(layer, x1, *ffn_w, None, alpha, tp["ffn"], t)
        tiles_per_seq = t // tp["ffn"]
        conv_p = tail.reshape(b, tiles_per_seq, V7X_SUBLANES, 2 * D_FF)[:, -1, -(CONV_W - 1):, :]
        for lst, v in zip(outs_p, (ckv.reshape(b, t, KV_RANK), kpe.reshape(b, t, MLA_ROPE),
                                   u.reshape(b, t, D_BR)[:, -POOL_BUF:], s_fin, conv_p)):
            lst.append(v)

        (u, qa, kk, lf, vv, sg, q_lat, q_pe, ckv, kpe, _) = _in_proj(
            layer, xs, *proj_w, cos_s, sin_s, tsm["in_proj"])
        ext = jnp.concatenate([state_pool[l], u.reshape(sb, ts, D_BR)], axis=1)
        y_pool = _pool_sample(layer, jnp.swapaxes(ext, 0, 1), pool_wb, pool_scale, ts, past_len)
        y_pool = jnp.swapaxes(y_pool, 0, 1).reshape(ns_, D_BR)
        o_hg, s_fin = _hgrn(layer, qa, kk, lf, vv, sg, hg_norm, state_hgrn, sb, ts,
                            min(HGRN_SHORT_TILE, ns_), ts)
        k_new = jnp.concatenate([ckv, kpe], axis=-1).reshape(sb, ts, D_QK)
        o_lat = _attn_sample(layer, page_table, _heads_q(q_lat, q_pe).reshape(sb, ts * MLA_HEADS, D_QK),
                             k_new, ckv.reshape(sb, ts, KV_RANK), cache_ckv, cache_kpe_t)
        x1 = _merge(layer, xs, y_pool, o_hg, o_lat.reshape(ns_, MLA_HEADS * KV_RANK), *merge_w, alpha,
                    tsm["merge"])
        xs, up = _ffn(layer, x1, *ffn_w, state_conv, alpha, tsm["ffn"], ts)
        conv_s = jnp.concatenate([state_conv[l], up.reshape(sb, ts, 2 * D_FF)], axis=1)[:, -(CONV_W - 1):]
        for lst, v in zip(outs_s, (ckv.reshape(sb, ts, KV_RANK), kpe.reshape(sb, ts, MLA_ROPE),
                                   ext[:, -POOL_BUF:], s_fin, conv_s)):
            lst.append(v)

    stack = lambda lists: [jnp.stack(v) for v in lists]
    return (xp.reshape(b, t, D_MODEL), xs.reshape(sb, ts, D_MODEL), *stack(outs_p), *stack(outs_s))


def kernel(x_prompt, x_sample, cache_ckv, cache_kpe, state_pool, state_hgrn, state_conv, page_table, w_in, pool_w, pool_scale, hg_lower_bounds, hg_norm, q_norm, w_uq, kv_norm, w_uk, w_uv, w_br_pool, w_br_hg, w_br_mla, w_out, ln1_g, ln1_b, ffn_w_up, ffn_conv_w, ffn_conv_b, ffn_w_down, ln2_g, ln2_b):
    return _trunk(x_prompt, x_sample, cache_ckv, cache_kpe, state_pool, state_hgrn, state_conv, page_table,
                  w_in, pool_w, pool_scale, hg_lower_bounds, hg_norm, q_norm, w_uq, kv_norm, w_uk, w_uv,
                  w_br_pool, w_br_hg, w_br_mla, w_out, ln1_g, ln1_b, ffn_w_up, ffn_conv_w, ffn_conv_b,
                  ffn_w_down, ln2_g, ln2_b)
```

```python
import functools

import jax
import jax.numpy as jnp
from jax import lax
from jax.experimental import pallas as pl
from jax.experimental.pallas import tpu as pltpu

F32 = jnp.float32
BF16 = jnp.bfloat16

D_MODEL = 1024
POOL_WINDOWS = (2, 4, 8, 16)
SEG = 128
N_SEG = 4
D_BR = N_SEG * SEG
POOL_BUF = max(POOL_WINDOWS) - 1
MLA_HEADS = 8
MLA_NOPE = 64
MLA_ROPE = 32
ROPE_HALF = MLA_ROPE // 2
MLA_V = 64
Q_RANK = 256
KV_RANK = SEG
D_QK = KV_RANK + MLA_ROPE
VT_ROWS = KV_RANK + 16
D_FF = 2816
CONV_W = 3
MLA_SCALE = (MLA_NOPE + MLA_ROPE) ** -0.5
LOG2_E = 1.4426950408889634
Q_SCALE = MLA_SCALE * LOG2_E
ROPE_THETA = 10000.0
LN_EPS = 1e-5
RMS_EPS = 1e-6

V7X_VMEM_BYTES = 64 * 1024 * 1024
V7X_LANES = 128
V7X_SUBLANES = 8
VMEM_LIMIT = V7X_VMEM_BYTES - 6 * 1024 * 1024

C_POOL, C_HQ, C_HF, C_HI, C_HG, C_CQ, C_CKV, C_K1, C_K2, C_END = (
    0, 512, 1024, 1536, 2048, 2560, 2816, 2944, 3072, 3200)
C_KPE_SRC = 2944
C_GATE_SRC = C_KPE_SRC + MLA_ROPE

NEG_BIG = -1e30
HGRN_SAFE_LOG_DECAY = -60.0


def _cparams(sem):
    return pltpu.CompilerParams(dimension_semantics=sem, vmem_limit_bytes=VMEM_LIMIT)


def _const_spec(shape):
    nd = len(shape)
    return pl.BlockSpec(shape, lambda *_: (0,) * nd, pipeline_mode=pl.Buffered(1))


def _layer_spec(arr):
    rest = arr.shape[1:]
    return pl.BlockSpec((None,) + rest, lambda *a: (a[-1][0],) + (0,) * len(rest),
                        pipeline_mode=pl.Buffered(1))


def _layer_grid(grid, in_specs, out_specs, scratch_shapes=()):
    return pltpu.PrefetchScalarGridSpec(num_scalar_prefetch=1, grid=grid, in_specs=in_specs,
                                        out_specs=out_specs, scratch_shapes=list(scratch_shapes))


def _sigmoid(x):
    return 0.5 * (jnp.tanh(0.5 * x) + 1.0)


def _silu(x):
    half = 0.5 * x
    return half + half * jnp.tanh(half)


def _layer_norm(h, g, b):
    mu = jnp.mean(h, axis=-1, keepdims=True)
    d = h - mu
    var = jnp.mean(d * d, axis=-1, keepdims=True)
    return d * lax.rsqrt(var + LN_EPS) * g + b


def _in_proj_kernel(layer_ref, x_ref, w_ref, hglb_ref, qn_ref, kvn_ref, wq_ref, wuk_ref, cos_ref, sin_ref,
                    u_ref, qa_ref, kk_ref, lf_ref, vv_ref, sg_ref, qlat_ref, qpe_ref, ckv_ref, kpe_ref,
                    kcat_ref):
    xb = x_ref[...].astype(BF16)

    def seg(lo, hi):
        return jnp.dot(xb, w_ref[:, lo:hi], preferred_element_type=F32)

    u_ref[...] = seg(C_POOL, C_HQ)

    qa_ref[...] = _silu(seg(C_HQ, C_HF))

    layer = layer_ref[0]
    hb = hglb_ref[...]
    e = jnp.exp(hb - jnp.max(hb, axis=0, keepdims=True))
    row = lax.broadcasted_iota(jnp.int32, hb.shape, 0)
    lb = (jnp.sum(jnp.where((row >= 1) & (row <= layer), e, 0.0), axis=0, keepdims=True)
          / jnp.sum(e, axis=0, keepdims=True))

    fl = seg(C_HF, C_HI)
    log_sig = jnp.minimum(fl, 0.0) - jnp.log(1.0 + jnp.exp(-jnp.abs(fl)))
    a = jnp.log(lb)
    b = jnp.log1p(-lb) + log_sig
    lf_ref[...] = jnp.maximum(a, b) + jnp.log(1.0 + jnp.exp(-jnp.abs(a - b)))
    kk_ref[...] = (1.0 - lb) * _sigmoid(-fl)

    vv_ref[...] = seg(C_HI, C_HG)
    sg_ref[...] = _silu(seg(C_HG, C_CQ))

    cq = seg(C_CQ, C_CKV)
    cqn = cq * lax.rsqrt(jnp.mean(cq * cq, axis=-1, keepdims=True) + RMS_EPS) * qn_ref[...]
    q = jnp.dot(cqn.astype(BF16), wq_ref[...], preferred_element_type=F32)
    cos = cos_ref[...]
    sin = sin_ref[...]
    n_nope = MLA_HEADS * MLA_NOPE
    x1 = q[:, n_nope:n_nope + V7X_LANES]
    x2 = q[:, n_nope + V7X_LANES:]
    qpe_ref[:, :V7X_LANES] = ((x1 * cos - x2 * sin) * Q_SCALE).astype(BF16)
    qpe_ref[:, V7X_LANES:] = ((x1 * sin + x2 * cos) * Q_SCALE).astype(BF16)
    qlat = jnp.dot(q[:, :n_nope].astype(BF16), wuk_ref[...], preferred_element_type=F32)
    qlat_ref[...] = (qlat * Q_SCALE).astype(BF16)

    ckv = seg(C_CKV, C_K1)
    ckv = ckv * lax.rsqrt(jnp.mean(ckv * ckv, axis=-1, keepdims=True) + RMS_EPS) * kvn_ref[...]
    ckv_ref[...] = ckv
    kcat_ref[:, :KV_RANK] = ckv.astype(BF16)

    k1 = seg(C_K1, C_K2)
    k2 = seg(C_K2, C_END)
    lane = lax.broadcasted_iota(jnp.int32, sin.shape, 1)
    kr = k1 * cos + k2 * jnp.where(lane < ROPE_HALF, -sin, sin)
    kpe_ref[...] = kr[:, :MLA_ROPE]
    kcat_ref[:, KV_RANK:] = kr[:, :MLA_ROPE].astype(BF16)


def _in_proj(layer, x, wa, hglb, q_norm, kv_norm, wq, wuk, cos_tab, sin_tab, tm):
    n = x.shape[0]
    tab_blocks = cos_tab.shape[0] // tm
    row = lambda w: pl.BlockSpec((tm, w), lambda i, *_: (i, 0))
    tab = pl.BlockSpec((tm, V7X_LANES), lambda i, *_: (i % tab_blocks, 0))
    widths = (D_BR,) * 6 + (MLA_HEADS * KV_RANK, MLA_HEADS * MLA_ROPE, KV_RANK, MLA_ROPE, D_QK)
    dtypes = (F32,) * 6 + (BF16, BF16, F32, F32, BF16)
    grid_spec = _layer_grid(
        (n // tm,),
        [row(D_MODEL), _layer_spec(wa), _const_spec(hglb.shape), _layer_spec(q_norm), _layer_spec(kv_norm),
         _layer_spec(wq), _layer_spec(wuk), tab, tab],
        [row(w) for w in widths])
    return pl.pallas_call(
        _in_proj_kernel,
        grid_spec=grid_spec,
        out_shape=[jax.ShapeDtypeStruct((n, w), d) for w, d in zip(widths, dtypes)],
        compiler_params=_cparams(("arbitrary",)),
        name="in_proj",
    )(layer, x, wa, hglb, q_norm, kv_norm, wq, wuk, cos_tab, sin_tab)


def _pool_prompt_kernel(layer_ref, u_ref, w_ref, sc_ref, y_ref, ext_ref, *, tt):
    del layer_ref
    halo = 2 * V7X_SUBLANES
    j = pl.program_id(1)

    @pl.when(j == 0)
    def _():
        ext_ref[0:halo, :] = jnp.zeros((halo, D_BR), F32)

    u = u_ref[0]
    ext_ref[halo:halo + tt, :] = u
    pos = j * tt + lax.broadcasted_iota(jnp.int32, (tt, 1), 0)
    for g, win in enumerate(POOL_WINDOWS):
        c0 = g * SEG
        assert win & (win - 1) == 0 and win - 1 <= halo
        acc = ext_ref[:, c0:c0 + SEG]
        shift = 1
        while shift < win:
            acc = acc + pltpu.roll(acc, shift, 0)
            shift *= 2
        acc = acc[halo:, :]
        cnt = jnp.minimum(pos + 1, win).astype(F32)
        diff = (acc / cnt - u[:, c0:c0 + SEG]).astype(BF16)
        y = jnp.dot(diff, w_ref[g], preferred_element_type=F32) * sc_ref[:, c0:c0 + SEG]
        y_ref[0, :, c0:c0 + SEG] = y.astype(BF16)
    ext_ref[0:halo, :] = ext_ref[tt:tt + halo, :]


def _pool_prompt(layer, u, w_pool, scale, tt):
    b, t, _ = u.shape
    blk = pl.BlockSpec((1, tt, D_BR), lambda i, j, *_: (i, j, 0))
    return pl.pallas_call(
        functools.partial(_pool_prompt_kernel, tt=tt),
        grid_spec=_layer_grid((b, t // tt), [blk, _layer_spec(w_pool), _layer_spec(scale)], blk,
                              [pltpu.VMEM((tt + 2 * V7X_SUBLANES, D_BR), F32)]),
        out_shape=jax.ShapeDtypeStruct((b, t, D_BR), BF16),
        compiler_params=_cparams(("arbitrary", "arbitrary")),
        name="pool_prompt",
    )(layer, u, w_pool, scale)


def _pool_sample_kernel(layer_ref, ext_ref, w_ref, sc_ref, y_ref, *, t_new, past_len):
    del layer_ref
    s = ext_ref.shape[1]
    for g, win in enumerate(POOL_WINDOWS):
        c0 = g * SEG
        diffs = []
        for t in range(t_new):
            cur = ext_ref[POOL_BUF + t, :, c0:c0 + SEG]
            acc = cur
            for d in range(1, win):
                acc = acc + ext_ref[POOL_BUF + t - d, :, c0:c0 + SEG]
            cnt = float(min(past_len + t + 1, win))
            diffs.append(acc / cnt - cur)
        diff = jnp.concatenate(diffs, axis=0).astype(BF16)
        y = jnp.dot(diff, w_ref[g], preferred_element_type=F32) * sc_ref[:, c0:c0 + SEG]
        y_ref[:, :, c0:c0 + SEG] = y.reshape(t_new, s, SEG).astype(BF16)


def _pool_sample(layer, ext_tm, w_pool, scale, t_new, past_len):
    _, s, _ = ext_tm.shape
    return pl.pallas_call(
        functools.partial(_pool_sample_kernel, t_new=t_new, past_len=past_len),
        grid_spec=_layer_grid((1,), [_const_spec(ext_tm.shape), _layer_spec(w_pool), _layer_spec(scale)],
                              pl.BlockSpec((t_new, s, D_BR), lambda i, *_: (0, 0, 0))),
        out_shape=jax.ShapeDtypeStruct((t_new, s, D_BR), BF16),
        compiler_params=_cparams(("arbitrary",)),
        name="pool_sample",
    )(layer, ext_tm, w_pool, scale)


def _hgrn_kernel(layer_ref, qa_ref, kk_ref, lf_ref, vv_ref, sg_ref, nw_ref, *rest, c, n_chunks, chained):
    del layer_ref
    if chained:
        o_ref, sfin_ref, st_ref, lsc_ref, ksc_ref, ot_ref = rest
    else:
        s0_ref, o_ref, sfin_ref, st_ref, lsc_ref, ksc_ref, ot_ref = rest
    tile = c * n_chunks
    log2c = c.bit_length() - 1
    j = pl.program_id(1)

    if chained:
        @pl.when(j == 0)
        def _():
            st_ref[...] = jnp.zeros(st_ref.shape, F32)

    r_i = lax.broadcasted_iota(jnp.int32, (tile, tile), 0)
    c_i = lax.broadcasted_iota(jnp.int32, (tile, tile), 1)
    chunk_of_row = jnp.right_shift(r_i, log2c)
    blk_causal = (chunk_of_row == jnp.right_shift(c_i, log2c)) & (r_i >= c_i)
    tri = blk_causal.astype(BF16)

    chunk_of_row_seg = jnp.right_shift(lax.broadcasted_iota(jnp.int32, (tile, SEG), 0), log2c)
    chunk_of_col = jnp.right_shift(lax.broadcasted_iota(jnp.int32, (SEG, tile), 1), log2c)

    def nt(a, b):
        return lax.dot_general(a, b, (((1,), (1,)), ((), ())), preferred_element_type=F32)

    def per_chunk(x3):
        return jnp.broadcast_to(x3, (n_chunks, c, D_BR)).reshape(tile, D_BR)

    lf = lf_ref[...]
    hi = lf.astype(BF16)
    r1 = lf - hi.astype(F32)
    mid = r1.astype(BF16)
    lo = (r1 - mid.astype(F32)).astype(BF16)
    cum = (jnp.dot(tri, hi, preferred_element_type=F32) + jnp.dot(tri, mid, preferred_element_type=F32)
           + jnp.dot(tri, lo, preferred_element_type=F32))
    cum3 = cum.reshape(n_chunks, c, D_BR)
    tot = per_chunk(cum3[:, c - 1:c, :])
    qa = qa_ref[...]
    kk = kk_ref[...]
    vv = vv_ref[...]
    q_in = qa * jnp.exp(cum)
    k_out = kk * jnp.exp(tot - cum)
    dec = jnp.exp(tot)
    safe = jnp.min(cum) >= HGRN_SAFE_LOG_DECAY

    def fast():
        k_rel = kk * jnp.exp(-cum)
        return [nt(q_in[:, h * SEG:(h + 1) * SEG].astype(BF16), k_rel[:, h * SEG:(h + 1) * SEG].astype(BF16))
                for h in range(N_SEG)]

    def exact():
        lsc_ref[...] = cum3
        ksc_ref[...] = kk.reshape(n_chunks, c, D_BR)

        def col(s, accs):
            w = jnp.exp(jnp.minimum(cum - per_chunk(lsc_ref[:, pl.ds(s, 1), :]), 0.0))
            prod = qa * per_chunk(ksc_ref[:, pl.ds(s, 1), :]) * w
            here = c_i == chunk_of_row * c + s
            return [jnp.where(here, jnp.sum(prod[:, h * SEG:(h + 1) * SEG], axis=-1, keepdims=True), accs[h])
                    for h in range(N_SEG)]

        return lax.fori_loop(0, c, col, [jnp.zeros((tile, tile), F32)] * N_SEG)

    a_all = lax.cond(safe, fast, exact)

    for h in range(N_SEG):
        cols = slice(h * SEG, (h + 1) * SEG)
        a_h = jnp.where(blk_causal, a_all[h], 0.0).astype(BF16)
        o_intra = jnp.dot(a_h, vv[:, cols].astype(BF16), preferred_element_type=F32)
        k_h = k_out[:, cols]
        kexp = jnp.concatenate([jnp.where(chunk_of_row_seg == ci, k_h, 0.0) for ci in range(n_chunks)],
                               axis=1).astype(BF16)
        kv_all = jnp.dot(vv[:, cols].T.astype(BF16), kexp, preferred_element_type=F32)
        states = []
        st = st_ref[h] if chained else None
        for ci in range(n_chunks):
            if not chained:
                st = s0_ref[ci, h].T
            states.append(st)
            st = st * dec[ci * c:ci * c + 1, cols] + kv_all[:, ci * SEG:(ci + 1) * SEG]
            if not chained:
                sfin_ref[ci, h] = st.T
        if chained:
            st_ref[h] = st
        q_t = q_in[:, cols].T
        qexp_t = jnp.concatenate([jnp.where(chunk_of_col == ci, q_t, 0.0) for ci in range(n_chunks)],
                                 axis=0).astype(BF16)
        ot_ref[...] = jnp.dot(jnp.concatenate(states, axis=1).astype(BF16), qexp_t,
                              preferred_element_type=F32)
        o = o_intra + ot_ref[...].T
        on = o * lax.rsqrt(jnp.mean(o * o, axis=-1, keepdims=True) + RMS_EPS) * nw_ref[:, cols]
        o_ref[:, cols] = (on * sg_ref[:, cols]).astype(BF16)

    if chained:
        @pl.when(j == pl.num_programs(1) - 1)
        def _():
            for h in range(N_SEG):
                sfin_ref[0, h] = st_ref[h].T


def _hgrn(layer, qa, kk, lf, vv, sg, norm_w, states, n_seq, t_len, tile, c):
    n = qa.shape[0]
    assert c & (c - 1) == 0 and tile % c == 0
    n_chunks = tile // c
    chained = states is None
    if chained:
        steps = t_len // tile
        grid = (n_seq, steps)
        row = pl.BlockSpec((tile, D_BR), lambda i, j, *_: (i * steps + j, 0))
        out_st = pl.BlockSpec((1, N_SEG, SEG, SEG), lambda i, j, *_: (i, 0, 0, 0))
        st_in = []
    else:
        assert t_len == c and n_seq % n_chunks == 0
        grid = (n_seq // n_chunks, 1)
        row = pl.BlockSpec((tile, D_BR), lambda i, j, *_: (i, 0))
        out_st = pl.BlockSpec((n_chunks, N_SEG, SEG, SEG), lambda i, j, *_: (i, 0, 0, 0))
        st_in = [pl.BlockSpec((None, n_chunks, N_SEG, SEG, SEG), lambda i, j, lyr: (lyr[0], i, 0, 0, 0))]
    return pl.pallas_call(
        functools.partial(_hgrn_kernel, c=c, n_chunks=n_chunks, chained=chained),
        grid_spec=_layer_grid(grid, [row] * 5 + [_layer_spec(norm_w)] + st_in, [row, out_st],
                              [pltpu.VMEM((N_SEG, SEG, SEG), F32), pltpu.VMEM((n_chunks, c, D_BR), F32),
                               pltpu.VMEM((n_chunks, c, D_BR), F32), pltpu.VMEM((SEG, tile), F32)]),
        out_shape=[jax.ShapeDtypeStruct((n, D_BR), BF16),
                   jax.ShapeDtypeStruct((n_seq, N_SEG, SEG, SEG), F32)],
        compiler_params=_cparams(("arbitrary", "arbitrary")),
        name="hgrn",
    )(layer, qa, kk, lf, vv, sg, norm_w, *([] if chained else [states]))


def _attn_prompt_kernel(ql_ref, qp_ref, k_ref, vt_ref, o_ref, qt_ref, m_ref, acc_ref, *, tq, tk):
    qi = pl.program_id(1)
    cols = tq * MLA_HEADS

    qpt = qp_ref[...].astype(F32).T
    for h in range(MLA_HEADS):
        cs = slice(h * tq, (h + 1) * tq)
        qt_ref[0:KV_RANK, cs] = ql_ref[:, h * KV_RANK:(h + 1) * KV_RANK].astype(F32).T.astype(BF16)
        qt_ref[KV_RANK:KV_RANK + ROPE_HALF, cs] = qpt[h * ROPE_HALF:(h + 1) * ROPE_HALF].astype(BF16)
        qt_ref[KV_RANK + ROPE_HALF:D_QK, cs] = qpt[V7X_LANES + h * ROPE_HALF:
                                                  V7X_LANES + (h + 1) * ROPE_HALF].astype(BF16)

    m_ref[...] = jnp.full((1, cols), NEG_BIG, F32)
    acc_ref[...] = jnp.zeros(acc_ref.shape, F32)

    def step(ks, nk, diagonal):
        s = jnp.dot(k_ref[0, pl.ds(ks, nk), :], qt_ref[...], preferred_element_type=F32)
        if diagonal:
            key = lax.broadcasted_iota(jnp.int32, (nk, cols), 0)
            tok = lax.broadcasted_iota(jnp.int32, (nk, cols), 1) & (tq - 1)
            s = jnp.where(key <= tok, s, NEG_BIG)
        m_old = m_ref[...]
        m_new = jnp.maximum(m_old, jnp.max(s, axis=0, keepdims=True))
        p = jnp.exp2(s - m_new).astype(BF16)
        acc_ref[...] = jnp.exp2(m_old - m_new) * acc_ref[...] + jnp.dot(
            vt_ref[0, :, pl.ds(ks, nk)], p, preferred_element_type=F32)
        m_ref[...] = m_new

    q0 = qi * tq
    n_full = q0 // tk

    def body(jb, carry):
        step(pl.multiple_of(jb * tk, tk), tk, False)
        return carry

    def rest(jb, carry):
        step(pl.multiple_of(n_full * tk + jb * tq, tq), tq, False)
        return carry

    lax.fori_loop(0, n_full, body, 0)
    lax.fori_loop(0, (q0 - n_full * tk) // tq, rest, 0)
    step(pl.multiple_of(q0, tq), tq, True)
    o = acc_ref[0:KV_RANK, :] / acc_ref[KV_RANK:KV_RANK + 1, :]
    for h in range(MLA_HEADS):
        o_ref[:, h * KV_RANK:(h + 1) * KV_RANK] = o[:, h * tq:(h + 1) * tq].T.astype(BF16)


def _attn_prompt(q_lat, q_pe, k, vt, tq, tk):
    b, t, _ = k.shape
    assert t % tk == 0 and tk % tq == 0 and tq & (tq - 1) == 0
    nq = t // tq
    cols = tq * MLA_HEADS
    qrow = lambda w: pl.BlockSpec((tq, w), lambda i, j: (i * nq + j, 0))
    return pl.pallas_call(
        functools.partial(_attn_prompt_kernel, tq=tq, tk=tk),
        grid=(b, nq),
        in_specs=[qrow(MLA_HEADS * KV_RANK), qrow(MLA_HEADS * MLA_ROPE),
                  pl.BlockSpec((1, t, D_QK), lambda i, j: (i, 0, 0)),
                  pl.BlockSpec((1, VT_ROWS, t), lambda i, j: (i, 0, 0))],
        out_specs=qrow(MLA_HEADS * KV_RANK),
        out_shape=jax.ShapeDtypeStruct((b * t, MLA_HEADS * KV_RANK), BF16),
        scratch_shapes=[pltpu.VMEM((D_QK, cols), BF16), pltpu.VMEM((1, cols), F32),
                        pltpu.VMEM((VT_ROWS, cols), F32)],
        compiler_params=_cparams(("arbitrary", "arbitrary")),
        name="attn_prompt",
    )(q_lat, q_pe, k, vt)


def _attn_sample_kernel(layer_ref, pt_ref, q_ref, kn_ref, vn_ref, ckv_hbm, kpe_hbm, o_ref,
                        cbuf, pbuf, sem, *, n_pages, page, t_new):
    b = pl.program_id(0)
    nb = pl.num_programs(0)
    slot = b % 2
    layer = layer_ref[0]
    n_here = q_ref.shape[0]

    def page_copies(step, slot_, j, p):
        pg = pt_ref[step * n_here + j, p]
        dst = pl.ds(p * page, page)
        return (pltpu.make_async_copy(ckv_hbm.at[layer, pg], cbuf.at[slot_, j, dst, :], sem.at[0, slot_]),
                pltpu.make_async_copy(kpe_hbm.at[layer, pg], pbuf.at[slot_, j, :, dst], sem.at[1, slot_]))

    def start_all(step, slot_):
        for j in range(n_here):
            for p in range(n_pages):
                for cp in page_copies(step, slot_, j, p):
                    cp.start()

    @pl.when(b == 0)
    def _():
        start_all(0, 0)

    @pl.when(b + 1 < nb)
    def _():
        start_all(b + 1, 1 - slot)

    for j in range(n_here):
        for p in range(n_pages):
            for cp in page_copies(b, slot, j, p):
                cp.wait()

    nt = (((1,), (1,)), ((), ()))
    rows = t_new * MLA_HEADS
    tok = lax.broadcasted_iota(jnp.int32, (rows, t_new), 0) // MLA_HEADS
    key = lax.broadcasted_iota(jnp.int32, (rows, t_new), 1)
    for j in range(n_here):
        q = q_ref[j]
        kc = cbuf[slot, j].astype(BF16)
        s = (lax.dot_general(q[:, :KV_RANK], kc, nt, preferred_element_type=F32)
             + jnp.dot(q[:, KV_RANK:], pbuf[slot, j].astype(BF16), preferred_element_type=F32))
        s_new = lax.dot_general(q.astype(F32), kn_ref[j], nt, preferred_element_type=F32)
        s_new = jnp.where(key <= tok, s_new, NEG_BIG)
        m = jnp.maximum(jnp.max(s, axis=-1, keepdims=True), jnp.max(s_new, axis=-1, keepdims=True))
        p = jnp.exp2(s - m)
        p_new = jnp.exp2(s_new - m)
        denom = jnp.sum(p, axis=-1, keepdims=True) + jnp.sum(p_new, axis=-1, keepdims=True)
        o = (jnp.dot(p.astype(BF16), kc, preferred_element_type=F32)
             + jnp.dot(p_new, vn_ref[j], preferred_element_type=F32))
        o_ref[j] = (o / denom).astype(BF16)


def _attn_sample(layer, page_table, q, k_new, v_new, cache_ckv, cache_kpe_t):
    n_seq, n_pages = page_table.shape
    page = cache_ckv.shape[2]
    t_new = k_new.shape[1]
    rows = t_new * MLA_HEADS
    past = n_pages * page
    ns = SAMPLE_SEQS_PER_STEP if n_seq % SAMPLE_SEQS_PER_STEP == 0 else 1
    grid_spec = pltpu.PrefetchScalarGridSpec(
        num_scalar_prefetch=2,
        grid=(n_seq // ns,),
        in_specs=[pl.BlockSpec((ns, rows, D_QK), lambda i, *_: (i, 0, 0)),
                  pl.BlockSpec((ns, t_new, D_QK), lambda i, *_: (i, 0, 0)),
                  pl.BlockSpec((ns, t_new, KV_RANK), lambda i, *_: (i, 0, 0)),
                  pl.BlockSpec(memory_space=pl.ANY), pl.BlockSpec(memory_space=pl.ANY)],
        out_specs=pl.BlockSpec((ns, rows, KV_RANK), lambda i, *_: (i, 0, 0)),
        scratch_shapes=[pltpu.VMEM((2, ns, past, KV_RANK), F32), pltpu.VMEM((2, ns, MLA_ROPE, past), F32),
                        pltpu.SemaphoreType.DMA((2, 2))],
    )
    return pl.pallas_call(
        functools.partial(_attn_sample_kernel, n_pages=n_pages, page=page, t_new=t_new),
        grid_spec=grid_spec,
        out_shape=jax.ShapeDtypeStruct((n_seq, rows, KV_RANK), BF16),
        compiler_params=_cparams(("arbitrary",)),
        name="attn_sample",
    )(layer, page_table, q, k_new, v_new, cache_ckv, cache_kpe_t)


def _merge_kernel(layer_ref, x_ref, yp_ref, oh_ref, ol_ref, wg_ref, wuv_ref, wp_ref, wh_ref, wm_ref, wo_ref,
                  g_ref, b_ref, y_ref, *, alpha):
    del layer_ref
    dot = functools.partial(jnp.dot, preferred_element_type=F32)
    x = x_ref[...]
    xb = x.astype(BF16)

    def gate(i):
        return _sigmoid(dot(xb, wg_ref[:, i * D_MODEL:(i + 1) * D_MODEL]))

    o_mla = dot(ol_ref[...], wuv_ref[...]).astype(BF16)
    merged = (gate(0) * dot(yp_ref[...], wp_ref[...]) + gate(1) * dot(oh_ref[...], wh_ref[...])
              + gate(2) * dot(o_mla, wm_ref[...]))
    h = alpha * x + dot(merged.astype(BF16), wo_ref[...])
    y_ref[...] = _layer_norm(h, g_ref[...], b_ref[...])


def _merge(layer, x, y_pool, o_hg, o_lat, wg, wuv, wp, wh, wm, wo, g, b, alpha, tm):
    n = x.shape[0]
    row = lambda w: pl.BlockSpec((tm, w), lambda i, *_: (i, 0))
    consts = (wg, wuv, wp, wh, wm, wo, g, b)
    return pl.pallas_call(
        functools.partial(_merge_kernel, alpha=alpha),
        grid_spec=_layer_grid((n // tm,),
                              [row(D_MODEL), row(D_BR), row(D_BR), row(MLA_HEADS * KV_RANK)]
                              + [_layer_spec(a) for a in consts], row(D_MODEL)),
        out_shape=jax.ShapeDtypeStruct((n, D_MODEL), F32),
        compiler_params=_cparams(("arbitrary",)),
        name="merge",
    )(layer, x, y_pool, o_hg, o_lat, *consts)


def _ffn_kernel(layer_ref, x_ref, wup_ref, cw_ref, cb_ref, wdn_ref, g_ref, b_ref, *rest, alpha, tm, seq_rows):
    del layer_ref
    long_seq = seq_rows % tm == 0
    if long_seq:
        y_ref, tail_ref, halo_ref, act_ref = rest
    else:
        st_ref, y_ref, tail_ref, halo_ref, act_ref = rest
    x = x_ref[...]
    xb = x.astype(BF16)
    sub = V7X_SUBLANES
    if long_seq:
        @pl.when(pl.program_id(0) % (seq_rows // tm) == 0)
        def _():
            halo_ref[...] = jnp.zeros(halo_ref.shape, F32)

        r8 = lax.broadcasted_iota(jnp.int32, (sub, 1), 0)
    else:
        n_s = tm // seq_rows
        t = lax.broadcasted_iota(jnp.int32, (tm, 1), 0) % seq_rows

    def conv_cols(lo):
        cs = slice(lo, lo + FFN_COL_BLOCK)
        up = jnp.dot(xb, wup_ref[:, cs], preferred_element_type=F32)
        prev1 = pltpu.roll(up, 1, 0)
        prev2 = pltpu.roll(up, 2, 0)
        if long_seq:
            last = halo_ref[sub - 1:sub, cs]
            last2 = halo_ref[sub - 2:sub - 1, cs]
            head1 = jnp.where(r8 == 0, last, prev1[:sub])
            head2 = jnp.where(r8 == 0, last2, jnp.where(r8 == 1, last, prev2[:sub]))
            prev1 = jnp.concatenate([head1, prev1[sub:]], axis=0)
            prev2 = jnp.concatenate([head2, prev2[sub:]], axis=0)
            halo_ref[:, cs] = up[tm - sub:, :]
            tail_ref[0, :, cs] = up[tm - sub:, :]
        else:
            st = st_ref[:, :, cs]
            wide = (n_s, seq_rows, FFN_COL_BLOCK)
            last = jnp.broadcast_to(st[:, 1:2, :], wide).reshape(tm, FFN_COL_BLOCK)
            last2 = jnp.broadcast_to(st[:, 0:1, :], wide).reshape(tm, FFN_COL_BLOCK)
            prev1 = jnp.where(t == 0, last, prev1)
            prev2 = jnp.where(t == 0, last2, jnp.where(t == 1, last, prev2))
            tail_ref[:, cs] = up
        return cb_ref[:, cs] + prev2 * cw_ref[0:1, cs] + prev1 * cw_ref[1:2, cs] + up * cw_ref[2:3, cs]

    for jb in range(D_FF // FFN_COL_BLOCK):
        lo = jb * FFN_COL_BLOCK
        act_ref[:, lo:lo + FFN_COL_BLOCK] = (_silu(conv_cols(lo)) * conv_cols(D_FF + lo)).astype(BF16)
    f = jnp.dot(act_ref[...], wdn_ref[...], preferred_element_type=F32)
    y_ref[...] = _layer_norm(alpha * x + f, g_ref[...], b_ref[...])


def _ffn(layer, x, wup, cw, cb, wdn, g, b, states, alpha, tm, seq_rows):
    n = x.shape[0]
    row = pl.BlockSpec((tm, D_MODEL), lambda i, *_: (i, 0))
    long_seq = seq_rows % tm == 0
    assert long_seq == (states is None)
    if long_seq:
        st_in = []
        tail_spec = pl.BlockSpec((1, V7X_SUBLANES, 2 * D_FF), lambda i, *_: (i, 0, 0))
        tail_shape = jax.ShapeDtypeStruct((n // tm, V7X_SUBLANES, 2 * D_FF), F32)
    else:
        n_s = tm // seq_rows
        st_in = [pl.BlockSpec((None, n_s, CONV_W - 1, 2 * D_FF), lambda i, lyr: (lyr[0], i, 0, 0))]
        tail_spec = pl.BlockSpec((tm, 2 * D_FF), lambda i, *_: (i, 0))
        tail_shape = jax.ShapeDtypeStruct((n, 2 * D_FF), F32)
    consts = (wup, cw, cb, wdn, g, b)
    return pl.pallas_call(
        functools.partial(_ffn_kernel, alpha=alpha, tm=tm, seq_rows=seq_rows),
        grid_spec=_layer_grid((n // tm,), [row] + [_layer_spec(a) for a in consts] + st_in,
                              [row, tail_spec],
                              [pltpu.VMEM((V7X_SUBLANES, 2 * D_FF), F32), pltpu.VMEM((tm, D_FF), BF16)]),
        out_shape=[jax.ShapeDtypeStruct((n, D_MODEL), F32), tail_shape],
        compiler_params=_cparams(("arbitrary",)),
        name="ffn",
    )(layer, x, *consts, *([] if long_seq else [states]))


def _pack_weights(w_in, w_uq, w_uk, w_uv):
    depth = w_in.shape[0]
    x1 = w_in[:, :, C_KPE_SRC:C_KPE_SRC + ROPE_HALF]
    x2 = w_in[:, :, C_KPE_SRC + ROPE_HALF:C_GATE_SRC]
    pad = jnp.zeros((depth, D_MODEL, V7X_LANES - MLA_ROPE), w_in.dtype)
    wa = jnp.concatenate([w_in[:, :, :C_KPE_SRC], x1, x2, pad, x2, x1, pad], axis=-1).astype(BF16)
    wg = w_in[:, :, C_GATE_SRC:].astype(BF16)
    uq = w_uq.reshape(depth, Q_RANK, MLA_HEADS, MLA_NOPE + MLA_ROPE)
    wq = jnp.concatenate([
        uq[..., :MLA_NOPE].reshape(depth, Q_RANK, -1),
        uq[..., MLA_NOPE:MLA_NOPE + ROPE_HALF].reshape(depth, Q_RANK, -1),
        uq[..., MLA_NOPE + ROPE_HALF:].reshape(depth, Q_RANK, -1)], axis=-1).astype(BF16)
    eye = jnp.eye(MLA_HEADS, dtype=w_uk.dtype)
    uk = jnp.transpose(w_uk, (0, 2, 3, 1))
    wuk = (eye[None, :, None, :, None] * uk[:, :, :, None, :]).reshape(
        depth, MLA_HEADS * MLA_NOPE, MLA_HEADS * KV_RANK).astype(BF16)
    uv = jnp.transpose(w_uv, (0, 2, 1, 3))
    wuv = (eye[None, :, None, :, None] * uv[:, :, :, None, :]).reshape(
        depth, MLA_HEADS * KV_RANK, MLA_HEADS * MLA_V).astype(BF16)
    return wa, wg, wq, wuk, wuv


def _rope_tables(pos):
    inv = ROPE_THETA ** (-jnp.arange(ROPE_HALF, dtype=F32) / ROPE_HALF)
    ang = pos.astype(F32)[:, None] * inv[None, :]
    reps = V7X_LANES // ROPE_HALF
    return jnp.tile(jnp.cos(ang), (1, reps)), jnp.tile(jnp.sin(ang), (1, reps))


def _heads_q(q_lat, q_pe):
    n = q_lat.shape[0]
    pe = q_pe.reshape(n, 2, MLA_HEADS, ROPE_HALF).transpose(0, 2, 1, 3).reshape(n, MLA_HEADS, MLA_ROPE)
    return jnp.concatenate([q_lat.reshape(n, MLA_HEADS, KV_RANK), pe], axis=-1).reshape(n * MLA_HEADS, D_QK)


def _tiles(n_rows, short_seqs):
    return dict(in_proj=min(512, n_rows), merge=min(512, n_rows),
                ffn=min(256 if short_seqs else 512, n_rows))


FFN_COL_BLOCK = 256


POOL_TILE = 512
HGRN_TILE = 256
HGRN_CHUNK = 32
HGRN_SHORT_TILE = 128
ATTN_TQ = 256
ATTN_TK = 512
SAMPLE_SEQS_PER_STEP = 1


def _trunk(x_prompt, x_sample, cache_ckv, cache_kpe, state_pool, state_hgrn, state_conv, page_table,
           w_in, pool_w, pool_scale, hg_lower_bounds, hg_norm, q_norm, w_uq, kv_norm, w_uk, w_uv,
           w_br_pool, w_br_hg, w_br_mla, w_out, ln1_g, ln1_b, ffn_w_up, ffn_conv_w, ffn_conv_b,
           ffn_w_down, ln2_g, ln2_b):
    depth = w_in.shape[0]
    b, t, _ = x_prompt.shape
    sb, ts, _ = x_sample.shape
    past_len = page_table.shape[1] * cache_ckv.shape[2]
    alpha = (2 * depth) ** 0.25

    wa, wg, wq, wuk, wuv = _pack_weights(w_in, w_uq, w_uk, w_uv)
    bf = lambda a: a.astype(BF16)
    pool_wb, wp, wh, wm, wo, wup, wdn = map(bf, (pool_w, w_br_pool, w_br_hg, w_br_mla, w_out, ffn_w_up,
                                                  ffn_w_down))
    row2 = lambda a: a.reshape(depth, 1, -1)
    pool_scale, hg_norm, q_norm, kv_norm, ln1_g, ln1_b, ln2_g, ln2_b, ffn_conv_b = map(
        row2, (pool_scale, hg_norm, q_norm, kv_norm, ln1_g, ln1_b, ln2_g, ln2_b, ffn_conv_b))

    np_, ns_ = b * t, sb * ts
    tp, tsm = _tiles(np_, False), _tiles(ns_, True)
    cos_p, sin_p = _rope_tables(jnp.arange(t, dtype=jnp.int32))
    cos_s, sin_s = _rope_tables(past_len + jnp.arange(tsm["in_proj"], dtype=jnp.int32) % ts)

    xp = x_prompt.reshape(np_, D_MODEL)
    xs = x_sample.reshape(ns_, D_MODEL)
    cache_kpe_t = jnp.swapaxes(cache_kpe, 2, 3)
    ones_rows = jnp.zeros((b, VT_ROWS - KV_RANK, t), BF16).at[:, 0, :].set(1.0)
    outs_p = [[] for _ in range(5)]
    outs_s = [[] for _ in range(5)]
    proj_w = (wa, hg_lower_bounds, q_norm, kv_norm, wq, wuk)
    merge_w = (wg, wuv, wp, wh, wm, wo, ln1_g, ln1_b)
    ffn_w = (wup, ffn_conv_w, ffn_conv_b, wdn, ln2_g, ln2_b)

    for l in range(depth):
        layer = jnp.full((1,), l, jnp.int32)

        (u, qa, kk, lf, vv, sg, q_lat, q_pe, ckv, kpe, kcat) = _in_proj(
            layer, xp, *proj_w, cos_p, sin_p, tp["in_proj"])
        y_pool = _pool_prompt(layer, u.reshape(b, t, D_BR), pool_wb, pool_scale, min(POOL_TILE, t))
        o_hg, s_fin = _hgrn(layer, qa, kk, lf, vv, sg, hg_norm, None, b, t, min(HGRN_TILE, t),
                            min(HGRN_CHUNK, t))
        kcat = kcat.reshape(b, t, D_QK)
        vt = jnp.concatenate([jnp.swapaxes(kcat[:, :, :KV_RANK], 1, 2), ones_rows], axis=1)
        o_lat = _attn_prompt(q_lat, q_pe, kcat, vt, min(ATTN_TQ, t), min(ATTN_TK, t))
        x1 = _merge(layer, xp, y_pool.reshape(np_, D_BR), o_hg, o_lat, *merge_w, alpha, tp["merge"])
        xp, tail = _ffn(layer, x1, *ffn_w, None, alpha, tp["ffn"], t)
        tiles_per_seq = t // tp["ffn"]
        conv_p = tail.reshape(b, tiles_per_seq, V7X_SUBLANES, 2 * D_FF)[:, -1, -(CONV_W - 1):, :]
        for lst, v in zip(outs_p, (ckv.reshape(b, t, KV_RANK), kpe.reshape(b, t, MLA_ROPE),
                                   u.reshape(b, t, D_BR)[:, -POOL_BUF:], s_fin, conv_p)):
            lst.append(v)

        (u, qa, kk, lf, vv, sg, q_lat, q_pe, ckv, kpe, _) = _in_proj(
            layer, xs, *proj_w, cos_s, sin_s, tsm["in_proj"])
        ext = jnp.concatenate([state_pool[l], u.reshape(sb, ts, D_BR)], axis=1)
        y_pool = _pool_sample(layer, jnp.swapaxes(ext, 0, 1), pool_wb, pool_scale, ts, past_len)
        y_pool = jnp.swapaxes(y_pool, 0, 1).reshape(ns_, D_BR)
        o_hg, s_fin = _hgrn(layer, qa, kk, lf, vv, sg, hg_norm, state_hgrn, sb, ts,
                            min(HGRN_SHORT_TILE, ns_), ts)
        k_new = jnp.concatenate([ckv, kpe], axis=-1).reshape(sb, ts, D_QK)
        o_lat = _attn_sample(layer, page_table, _heads_q(q_lat, q_pe).reshape(sb, ts * MLA_HEADS, D_QK),
                             k_new, ckv.reshape(sb, ts, KV_RANK), cache_ckv, cache_kpe_t)
        x1 = _merge(layer, xs, y_pool, o_hg, o_lat.reshape(ns_, MLA_HEADS * KV_RANK), *merge_w, alpha,
                    tsm["merge"])
        xs, up = _ffn(layer, x1, *ffn_w, state_conv, alpha, tsm["ffn"], ts)
        conv_s = jnp.concatenate([state_conv[l], up.reshape(sb, ts, 2 * D_FF)], axis=1)[:, -(CONV_W - 1):]
        for lst, v in zip(outs_s, (ckv.reshape(sb, ts, KV_RANK), kpe.reshape(sb, ts, MLA_ROPE),
                                   ext[:, -POOL_BUF:], s_fin, conv_s)):
            lst.append(v)

    stack = lambda lists: [jnp.stack(v) for v in lists]
    return (xp.reshape(b, t, D_MODEL), xs.reshape(sb, ts, D_MODEL), *stack(outs_p), *stack(outs_s))


def kernel(x_prompt, x_sample, cache_ckv, cache_kpe, state_pool, state_hgrn, state_conv, page_table, w_in, pool_w, pool_scale, hg_lower_bounds, hg_norm, q_norm, w_uq, kv_norm, w_uk, w_uv, w_br_pool, w_br_hg, w_br_mla, w_out, ln1_g, ln1_b, ffn_w_up, ffn_conv_w, ffn_conv_b, ffn_w_down, ln2_g, ln2_b):
    return _trunk(x_prompt, x_sample, cache_ckv, cache_kpe, state_pool, state_hgrn, state_conv, page_table,
                  w_in, pool_w, pool_scale, hg_lower_bounds, hg_norm, q_norm, w_uq, kv_norm, w_uk, w_uv,
                  w_br_pool, w_br_hg, w_br_mla, w_out, ln1_g, ln1_b, ffn_w_up, ffn_conv_w, ffn_conv_b,
                  ffn_w_down, ln2_g, ln2_b)
```

```python
import functools

import jax
import jax.numpy as jnp
from jax import lax
from jax.experimental import pallas as pl
from jax.experimental.pallas import tpu as pltpu

F32 = jnp.float32
BF16 = jnp.bfloat16

D_MODEL = 1024
POOL_WINDOWS = (2, 4, 8, 16)
SEG = 128
N_SEG = 4
D_BR = N_SEG * SEG
POOL_BUF = max(POOL_WINDOWS) - 1
MLA_HEADS = 8
MLA_NOPE = 64
MLA_ROPE = 32
ROPE_HALF = MLA_ROPE // 2
MLA_V = 64
Q_RANK = 256
KV_RANK = SEG
D_QK = KV_RANK + MLA_ROPE
VT_ROWS = KV_RANK + 16
D_FF = 2816
CONV_W = 3
MLA_SCALE = (MLA_NOPE + MLA_ROPE) ** -0.5
LOG2_E = 1.4426950408889634
Q_SCALE = MLA_SCALE * LOG2_E
ROPE_THETA = 10000.0
LN_EPS = 1e-5
RMS_EPS = 1e-6

V7X_VMEM_BYTES = 64 * 1024 * 1024
V7X_LANES = 128
V7X_SUBLANES = 8
VMEM_LIMIT = V7X_VMEM_BYTES - 6 * 1024 * 1024

C_POOL, C_HQ, C_HF, C_HI, C_HG, C_CQ, C_CKV, C_K1, C_K2, C_END = (
    0, 512, 1024, 1536, 2048, 2560, 2816, 2944, 3072, 3200)
C_KPE_SRC = 2944
C_GATE_SRC = C_KPE_SRC + MLA_ROPE

NEG_BIG = -1e30
HGRN_SAFE_LOG_DECAY = -60.0


def _cparams(sem):
    return pltpu.CompilerParams(dimension_semantics=sem, vmem_limit_bytes=VMEM_LIMIT)


def _const_spec(shape):
    nd = len(shape)
    return pl.BlockSpec(shape, lambda *_: (0,) * nd, pipeline_mode=pl.Buffered(1))


def _layer_spec(arr):
    rest = arr.shape[1:]
    return pl.BlockSpec((None,) + rest, lambda *a: (a[-1][0],) + (0,) * len(rest),
                        pipeline_mode=pl.Buffered(1))


def _layer_grid(grid, in_specs, out_specs, scratch_shapes=()):
    return pltpu.PrefetchScalarGridSpec(num_scalar_prefetch=1, grid=grid, in_specs=in_specs,
                                        out_specs=out_specs, scratch_shapes=list(scratch_shapes))


def _sigmoid(x):
    return 0.5 * (jnp.tanh(0.5 * x) + 1.0)


def _silu(x):
    half = 0.5 * x
    return half + half * jnp.tanh(half)


def _layer_norm(h, g, b):
    mu = jnp.mean(h, axis=-1, keepdims=True)
    d = h - mu
    var = jnp.mean(d * d, axis=-1, keepdims=True)
    return d * lax.rsqrt(var + LN_EPS) * g + b


def _in_proj_kernel(layer_ref, x_ref, w_ref, hglb_ref, qn_ref, kvn_ref, wq_ref, wuk_ref, cos_ref, sin_ref,
                    u_ref, qa_ref, kk_ref, lf_ref, vv_ref, sg_ref, qlat_ref, qpe_ref, ckv_ref, kpe_ref,
                    kcat_ref):
    xb = x_ref[...].astype(BF16)

    def seg(lo, hi):
        return jnp.dot(xb, w_ref[:, lo:hi], preferred_element_type=F32)

    u_ref[...] = seg(C_POOL, C_HQ)

    qa_ref[...] = _silu(seg(C_HQ, C_HF))

    layer = layer_ref[0]
    hb = hglb_ref[...]
    e = jnp.exp(hb - jnp.max(hb, axis=0, keepdims=True))
    row = lax.broadcasted_iota(jnp.int32, hb.shape, 0)
    lb = (jnp.sum(jnp.where((row >= 1) & (row <= layer), e, 0.0), axis=0, keepdims=True)
          / jnp.sum(e, axis=0, keepdims=True))

    fl = seg(C_HF, C_HI)
    log_sig = jnp.minimum(fl, 0.0) - jnp.log(1.0 + jnp.exp(-jnp.abs(fl)))
    a = jnp.log(lb)
    b = jnp.log1p(-lb) + log_sig
    lf_ref[...] = jnp.maximum(a, b) + jnp.log(1.0 + jnp.exp(-jnp.abs(a - b)))
    kk_ref[...] = (1.0 - lb) * _sigmoid(-fl)

    vv_ref[...] = seg(C_HI, C_HG)
    sg_ref[...] = _silu(seg(C_HG, C_CQ))

    cq = seg(C_CQ, C_CKV)
    cqn = cq * lax.rsqrt(jnp.mean(cq * cq, axis=-1, keepdims=True) + RMS_EPS) * qn_ref[...]
    q = jnp.dot(cqn.astype(BF16), wq_ref[...], preferred_element_type=F32)
    cos = cos_ref[...]
    sin = sin_ref[...]
    n_nope = MLA_HEADS * MLA_NOPE
    x1 = q[:, n_nope:n_nope + V7X_LANES]
    x2 = q[:, n_nope + V7X_LANES:]
    qpe_ref[:, :V7X_LANES] = ((x1 * cos - x2 * sin) * Q_SCALE).astype(BF16)
    qpe_ref[:, V7X_LANES:] = ((x1 * sin + x2 * cos) * Q_SCALE).astype(BF16)
    qlat = jnp.dot(q[:, :n_nope].astype(BF16), wuk_ref[...], preferred_element_type=F32)
    qlat_ref[...] = (qlat * Q_SCALE).astype(BF16)

    ckv = seg(C_CKV, C_K1)
    ckv = ckv * lax.rsqrt(jnp.mean(ckv * ckv, axis=-1, keepdims=True) + RMS_EPS) * kvn_ref[...]
    ckv_ref[...] = ckv
    kcat_ref[:, :KV_RANK] = ckv.astype(BF16)

    k1 = seg(C_K1, C_K2)
    k2 = seg(C_K2, C_END)
    lane = lax.broadcasted_iota(jnp.int32, sin.shape, 1)
    kr = k1 * cos + k2 * jnp.where(lane < ROPE_HALF, -sin, sin)
    kpe_ref[...] = kr[:, :MLA_ROPE]
    kcat_ref[:, KV_RANK:] = kr[:, :MLA_ROPE].astype(BF16)


def _in_proj(layer, x, wa, hglb, q_norm, kv_norm, wq, wuk, cos_tab, sin_tab, tm):
    n = x.shape[0]
    tab_blocks = cos_tab.shape[0] // tm
    row = lambda w: pl.BlockSpec((tm, w), lambda i, *_: (i, 0))
    tab = pl.BlockSpec((tm, V7X_LANES), lambda i, *_: (i % tab_blocks, 0))
    widths = (D_BR,) * 6 + (MLA_HEADS * KV_RANK, MLA_HEADS * MLA_ROPE, KV_RANK, MLA_ROPE, D_QK)
    dtypes = (F32,) * 6 + (BF16, BF16, F32, F32, BF16)
    grid_spec = _layer_grid(
        (n // tm,),
        [row(D_MODEL), _layer_spec(wa), _const_spec(hglb.shape), _layer_spec(q_norm), _layer_spec(kv_norm),
         _layer_spec(wq), _layer_spec(wuk), tab, tab],
        [row(w) for w in widths])
    return pl.pallas_call(
        _in_proj_kernel,
        grid_spec=grid_spec,
        out_shape=[jax.ShapeDtypeStruct((n, w), d) for w, d in zip(widths, dtypes)],
        compiler_params=_cparams(("arbitrary",)),
        name="in_proj",
    )(layer, x, wa, hglb, q_norm, kv_norm, wq, wuk, cos_tab, sin_tab)


def _pool_prompt_kernel(layer_ref, u_ref, w_ref, sc_ref, y_ref, ext_ref, *, tt):
    del layer_ref
    halo = 2 * V7X_SUBLANES
    j = pl.program_id(1)

    @pl.when(j == 0)
    def _():
        ext_ref[0:halo, :] = jnp.zeros((halo, D_BR), F32)

    u = u_ref[0]
    ext_ref[halo:halo + tt, :] = u
    pos = j * tt + lax.broadcasted_iota(jnp.int32, (tt, 1), 0)
    for g, win in enumerate(POOL_WINDOWS):
        c0 = g * SEG
        assert win & (win - 1) == 0 and win - 1 <= halo
        acc = ext_ref[:, c0:c0 + SEG]
        shift = 1
        while shift < win:
            acc = acc + pltpu.roll(acc, shift, 0)
            shift *= 2
        acc = acc[halo:, :]
        cnt = jnp.minimum(pos + 1, win).astype(F32)
        diff = (acc / cnt - u[:, c0:c0 + SEG]).astype(BF16)
        y = jnp.dot(diff, w_ref[g], preferred_element_type=F32) * sc_ref[:, c0:c0 + SEG]
        y_ref[0, :, c0:c0 + SEG] = y.astype(BF16)
    ext_ref[0:halo, :] = ext_ref[tt:tt + halo, :]


def _pool_prompt(layer, u, w_pool, scale, tt):
    b, t, _ = u.shape
    blk = pl.BlockSpec((1, tt, D_BR), lambda i, j, *_: (i, j, 0))
    return pl.pallas_call(
        functools.partial(_pool_prompt_kernel, tt=tt),
        grid_spec=_layer_grid((b, t // tt), [blk, _layer_spec(w_pool), _layer_spec(scale)], blk,
                              [pltpu.VMEM((tt + 2 * V7X_SUBLANES, D_BR), F32)]),
        out_shape=jax.ShapeDtypeStruct((b, t, D_BR), BF16),
        compiler_params=_cparams(("arbitrary", "arbitrary")),
        name="pool_prompt",
    )(layer, u, w_pool, scale)


def _pool_sample_kernel(layer_ref, ext_ref, w_ref, sc_ref, y_ref, *, t_new, past_len):
    del layer_ref
    s = ext_ref.shape[1]
    for g, win in enumerate(POOL_WINDOWS):
        c0 = g * SEG
        diffs = []
        for t in range(t_new):
            cur = ext_ref[POOL_BUF + t, :, c0:c0 + SEG]
            acc = cur
            for d in range(1, win):
                acc = acc + ext_ref[POOL_BUF + t - d, :, c0:c0 + SEG]
            cnt = float(min(past_len + t + 1, win))
            diffs.append(acc / cnt - cur)
        diff = jnp.concatenate(diffs, axis=0).astype(BF16)
        y = jnp.dot(diff, w_ref[g], preferred_element_type=F32) * sc_ref[:, c0:c0 + SEG]
        y_ref[:, :, c0:c0 + SEG] = y.reshape(t_new, s, SEG).astype(BF16)


def _pool_sample(layer, ext_tm, w_pool, scale, t_new, past_len):
    _, s, _ = ext_tm.shape
    return pl.pallas_call(
        functools.partial(_pool_sample_kernel, t_new=t_new, past_len=past_len),
        grid_spec=_layer_grid((1,), [_const_spec(ext_tm.shape), _layer_spec(w_pool), _layer_spec(scale)],
                              pl.BlockSpec((t_new, s, D_BR), lambda i, *_: (0, 0, 0))),
        out_shape=jax.ShapeDtypeStruct((t_new, s, D_BR), BF16),
        compiler_params=_cparams(("arbitrary",)),
        name="pool_sample",
    )(layer, ext_tm, w_pool, scale)


def _hgrn_kernel(layer_ref, qa_ref, kk_ref, lf_ref, vv_ref, sg_ref, nw_ref, *rest, c, n_chunks, chained):
    del layer_ref
    if chained:
        o_ref, sfin_ref, st_ref, lsc_ref, ksc_ref, ot_ref = rest
    else:
        s0_ref, o_ref, sfin_ref, st_ref, lsc_ref, ksc_ref, ot_ref = rest
    tile = c * n_chunks
    log2c = c.bit_length() - 1
    j = pl.program_id(1)

    if chained:
        @pl.when(j == 0)
        def _():
            st_ref[...] = jnp.zeros(st_ref.shape, F32)

    r_i = lax.broadcasted_iota(jnp.int32, (tile, tile), 0)
    c_i = lax.broadcasted_iota(jnp.int32, (tile, tile), 1)
    chunk_of_row = jnp.right_shift(r_i, log2c)
    blk_causal = (chunk_of_row == jnp.right_shift(c_i, log2c)) & (r_i >= c_i)
    tri = blk_causal.astype(BF16)

    chunk_of_row_seg = jnp.right_shift(lax.broadcasted_iota(jnp.int32, (tile, SEG), 0), log2c)
    chunk_of_col = jnp.right_shift(lax.broadcasted_iota(jnp.int32, (SEG, tile), 1), log2c)

    def nt(a, b):
        return lax.dot_general(a, b, (((1,), (1,)), ((), ())), preferred_element_type=F32)

    def per_chunk(x3):
        return jnp.broadcast_to(x3, (n_chunks, c, D_BR)).reshape(tile, D_BR)

    lf = lf_ref[...]
    hi = lf.astype(BF16)
    r1 = lf - hi.astype(F32)
    mid = r1.astype(BF16)
    lo = (r1 - mid.astype(F32)).astype(BF16)
    cum = (jnp.dot(tri, hi, preferred_element_type=F32) + jnp.dot(tri, mid, preferred_element_type=F32)
           + jnp.dot(tri, lo, preferred_element_type=F32))
    cum3 = cum.reshape(n_chunks, c, D_BR)
    tot = per_chunk(cum3[:, c - 1:c, :])
    qa = qa_ref[...]
    kk = kk_ref[...]
    vv = vv_ref[...]
    q_in = qa * jnp.exp(cum)
    k_out = kk * jnp.exp(tot - cum)
    dec = jnp.exp(tot)
    safe = jnp.min(cum) >= HGRN_SAFE_LOG_DECAY

    def fast():
        k_rel = kk * jnp.exp(-cum)
        return [nt(q_in[:, h * SEG:(h + 1) * SEG].astype(BF16), k_rel[:, h * SEG:(h + 1) * SEG].astype(BF16))
                for h in range(N_SEG)]

    def exact():
        lsc_ref[...] = cum3
        ksc_ref[...] = kk.reshape(n_chunks, c, D_BR)

        def col(s, accs):
            w = jnp.exp(jnp.minimum(cum - per_chunk(lsc_ref[:, pl.ds(s, 1), :]), 0.0))
            prod = qa * per_chunk(ksc_ref[:, pl.ds(s, 1), :]) * w
            here = c_i == chunk_of_row * c + s
            return [jnp.where(here, jnp.sum(prod[:, h * SEG:(h + 1) * SEG], axis=-1, keepdims=True), accs[h])
                    for h in range(N_SEG)]

        return lax.fori_loop(0, c, col, [jnp.zeros((tile, tile), F32)] * N_SEG)

    a_all = lax.cond(safe, fast, exact)

    for h in range(N_SEG):
        cols = slice(h * SEG, (h + 1) * SEG)
        a_h = jnp.where(blk_causal, a_all[h], 0.0).astype(BF16)
        o_intra = jnp.dot(a_h, vv[:, cols].astype(BF16), preferred_element_type=F32)
        k_h = k_out[:, cols]
        kexp = jnp.concatenate([jnp.where(chunk_of_row_seg == ci, k_h, 0.0) for ci in range(n_chunks)],
                               axis=1).astype(BF16)
        kv_all = jnp.dot(vv[:, cols].T.astype(BF16), kexp, preferred_element_type=F32)
        states = []
        st = st_ref[h] if chained else None
        for ci in range(n_chunks):
            if not chained:
                st = s0_ref[ci, h].T
            states.append(st)
            st = st * dec[ci * c:ci * c + 1, cols] + kv_all[:, ci * SEG:(ci + 1) * SEG]
            if not chained:
                sfin_ref[ci, h] = st.T
        if chained:
            st_ref[h] = st
        q_t = q_in[:, cols].T
        qexp_t = jnp.concatenate([jnp.where(chunk_of_col == ci, q_t, 0.0) for ci in range(n_chunks)],
                                 axis=0).astype(BF16)
        ot_ref[...] = jnp.dot(jnp.concatenate(states, axis=1).astype(BF16), qexp_t,
                              preferred_element_type=F32)
        o = o_intra + ot_ref[...].T
        on = o * lax.rsqrt(jnp.mean(o * o, axis=-1, keepdims=True) + RMS_EPS) * nw_ref[:, cols]
        o_ref[:, cols] = (on * sg_ref[:, cols]).astype(BF16)

    if chained:
        @pl.when(j == pl.num_programs(1) - 1)
        def _():
            for h in range(N_SEG):
                sfin_ref[0, h] = st_ref[h].T


def _hgrn(layer, qa, kk, lf, vv, sg, norm_w, states, n_seq, t_len, tile, c):
    n = qa.shape[0]
    assert c & (c - 1) == 0 and tile % c == 0
    n_chunks = tile // c
    chained = states is None
    if chained:
        steps = t_len // tile
        grid = (n_seq, steps)
        row = pl.BlockSpec((tile, D_BR), lambda i, j, *_: (i * steps + j, 0))
        out_st = pl.BlockSpec((1, N_SEG, SEG, SEG), lambda i, j, *_: (i, 0, 0, 0))
        st_in = []
    else:
        assert t_len == c and n_seq % n_chunks == 0
        grid = (n_seq // n_chunks, 1)
        row = pl.BlockSpec((tile, D_BR), lambda i, j, *_: (i, 0))
        out_st = pl.BlockSpec((n_chunks, N_SEG, SEG, SEG), lambda i, j, *_: (i, 0, 0, 0))
        st_in = [pl.BlockSpec((None, n_chunks, N_SEG, SEG, SEG), lambda i, j, lyr: (lyr[0], i, 0, 0, 0))]
    return pl.pallas_call(
        functools.partial(_hgrn_kernel, c=c, n_chunks=n_chunks, chained=chained),
        grid_spec=_layer_grid(grid, [row] * 5 + [_layer_spec(norm_w)] + st_in, [row, out_st],
                              [pltpu.VMEM((N_SEG, SEG, SEG), F32), pltpu.VMEM((n_chunks, c, D_BR), F32),
                               pltpu.VMEM((n_chunks, c, D_BR), F32), pltpu.VMEM((SEG, tile), F32)]),
        out_shape=[jax.ShapeDtypeStruct((n, D_BR), BF16),
                   jax.ShapeDtypeStruct((n_seq, N_SEG, SEG, SEG), F32)],
        compiler_params=_cparams(("arbitrary", "arbitrary")),
        name="hgrn",
    )(layer, qa, kk, lf, vv, sg, norm_w, *([] if chained else [states]))


def _attn_prompt_kernel(ql_ref, qp_ref, k_ref, vt_ref, o_ref, qt_ref, m_ref, acc_ref, *, tq, tk):
    qi = pl.program_id(1)
    cols = tq * MLA_HEADS

    qpt = qp_ref[...].astype(F32).T
    for h in range(MLA_HEADS):
        cs = slice(h * tq, (h + 1) * tq)
        qt_ref[0:KV_RANK, cs] = ql_ref[:, h * KV_RANK:(h + 1) * KV_RANK].astype(F32).T.astype(BF16)
        qt_ref[KV_RANK:KV_RANK + ROPE_HALF, cs] = qpt[h * ROPE_HALF:(h + 1) * ROPE_HALF].astype(BF16)
        qt_ref[KV_RANK + ROPE_HALF:D_QK, cs] = qpt[V7X_LANES + h * ROPE_HALF:
                                                  V7X_LANES + (h + 1) * ROPE_HALF].astype(BF16)

    m_ref[...] = jnp.full((1, cols), NEG_BIG, F32)
    acc_ref[...] = jnp.zeros(acc_ref.shape, F32)

    def step(ks, nk, diagonal):
        s = jnp.dot(k_ref[0, pl.ds(ks, nk), :], qt_ref[...], preferred_element_type=F32)
        if diagonal:
            key = lax.broadcasted_iota(jnp.int32, (nk, cols), 0)
            tok = lax.broadcasted_iota(jnp.int32, (nk, cols), 1) & (tq - 1)
            s = jnp.where(key <= tok, s, NEG_BIG)
        m_old = m_ref[...]
        m_new = jnp.maximum(m_old, jnp.max(s, axis=0, keepdims=True))
        p = jnp.exp2(s - m_new).astype(BF16)
        acc_ref[...] = jnp.exp2(m_old - m_new) * acc_ref[...] + jnp.dot(
            vt_ref[0, :, pl.ds(ks, nk)], p, preferred_element_type=F32)
        m_ref[...] = m_new

    q0 = qi * tq
    n_full = q0 // tk

    def body(jb, carry):
        step(pl.multiple_of(jb * tk, tk), tk, False)
        return carry

    def rest(jb, carry):
        step(pl.multiple_of(n_full * tk + jb * tq, tq), tq, False)
        return carry

    lax.fori_loop(0, n_full, body, 0)
    lax.fori_loop(0, (q0 - n_full * tk) // tq, rest, 0)
    step(pl.multiple_of(q0, tq), tq, True)
    o = acc_ref[0:KV_RANK, :] / acc_ref[KV_RANK:KV_RANK + 1, :]
    for h in range(MLA_HEADS):
        o_ref[:, h * KV_RANK:(h + 1) * KV_RANK] = o[:, h * tq:(h + 1) * tq].T.astype(BF16)


def _attn_prompt(q_lat, q_pe, k, vt, tq, tk):
    b, t, _ = k.shape
    assert t % tk == 0 and tk % tq == 0 and tq & (tq - 1) == 0
    nq = t // tq
    cols = tq * MLA_HEADS
    qrow = lambda w: pl.BlockSpec((tq, w), lambda i, j: (i * nq + j, 0))
    return pl.pallas_call(
        functools.partial(_attn_prompt_kernel, tq=tq, tk=tk),
        grid=(b, nq),
        in_specs=[qrow(MLA_HEADS * KV_RANK), qrow(MLA_HEADS * MLA_ROPE),
                  pl.BlockSpec((1, t, D_QK), lambda i, j: (i, 0, 0)),
                  pl.BlockSpec((1, VT_ROWS, t), lambda i, j: (i, 0, 0))],
        out_specs=qrow(MLA_HEADS * KV_RANK),
        out_shape=jax.ShapeDtypeStruct((b * t, MLA_HEADS * KV_RANK), BF16),
        scratch_shapes=[pltpu.VMEM((D_QK, cols), BF16), pltpu.VMEM((1, cols), F32),
                        pltpu.VMEM((VT_ROWS, cols), F32)],
        compiler_params=_cparams(("arbitrary", "arbitrary")),
        name="attn_prompt",
    )(q_lat, q_pe, k, vt)


def _attn_sample_kernel(layer_ref, pt_ref, q_ref, kn_ref, vn_ref, ckv_hbm, kpe_hbm, o_ref,
                        cbuf, pbuf, sem, *, n_pages, page, t_new):
    b = pl.program_id(0)
    nb = pl.num_programs(0)
    slot = b % 2
    layer = layer_ref[0]
    n_here = q_ref.shape[0]

    def page_copies(step, slot_, j, p):
        pg = pt_ref[step * n_here + j, p]
        dst = pl.ds(p * page, page)
        return (pltpu.make_async_copy(ckv_hbm.at[layer, pg], cbuf.at[slot_, j, dst, :], sem.at[0, slot_]),
                pltpu.make_async_copy(kpe_hbm.at[layer, pg], pbuf.at[slot_, j, :, dst], sem.at[1, slot_]))

    def start_all(step, slot_):
        for j in range(n_here):
            for p in range(n_pages):
                for cp in page_copies(step, slot_, j, p):
                    cp.start()

    @pl.when(b == 0)
    def _():
        start_all(0, 0)

    @pl.when(b + 1 < nb)
    def _():
        start_all(b + 1, 1 - slot)

    for j in range(n_here):
        for p in range(n_pages):
            for cp in page_copies(b, slot, j, p):
                cp.wait()

    nt = (((1,), (1,)), ((), ()))
    rows = t_new * MLA_HEADS
    tok = lax.broadcasted_iota(jnp.int32, (rows, t_new), 0) // MLA_HEADS
    key = lax.broadcasted_iota(jnp.int32, (rows, t_new), 1)
    for j in range(n_here):
        q = q_ref[j]
        kc = cbuf[slot, j].astype(BF16)
        s = (lax.dot_general(q[:, :KV_RANK], kc, nt, preferred_element_type=F32)
             + jnp.dot(q[:, KV_RANK:], pbuf[slot, j].astype(BF16), preferred_element_type=F32))
        s_new = lax.dot_general(q.astype(F32), kn_ref[j], nt, preferred_element_type=F32)
        s_new = jnp.where(key <= tok, s_new, NEG_BIG)
        m = jnp.maximum(jnp.max(s, axis=-1, keepdims=True), jnp.max(s_new, axis=-1, keepdims=True))
        p = jnp.exp2(s - m)
        p_new = jnp.exp2(s_new - m)
        denom = jnp.sum(p, axis=-1, keepdims=True) + jnp.sum(p_new, axis=-1, keepdims=True)
        o = (jnp.dot(p.astype(BF16), kc, preferred_element_type=F32)
             + jnp.dot(p_new, vn_ref[j], preferred_element_type=F32))
        o_ref[j] = (o / denom).astype(BF16)


def _attn_sample(layer, page_table, q, k_new, v_new, cache_ckv, cache_kpe_t):
    n_seq, n_pages = page_table.shape
    page = cache_ckv.shape[2]
    t_new = k_new.shape[1]
    rows = t_new * MLA_HEADS
    past = n_pages * page
    ns = SAMPLE_SEQS_PER_STEP if n_seq % SAMPLE_SEQS_PER_STEP == 0 else 1
    grid_spec = pltpu.PrefetchScalarGridSpec(
        num_scalar_prefetch=2,
        grid=(n_seq // ns,),
        in_specs=[pl.BlockSpec((ns, rows, D_QK), lambda i, *_: (i, 0, 0)),
                  pl.BlockSpec((ns, t_new, D_QK), lambda i, *_: (i, 0, 0)),
                  pl.BlockSpec((ns, t_new, KV_RANK), lambda i, *_: (i, 0, 0)),
                  pl.BlockSpec(memory_space=pl.ANY), pl.BlockSpec(memory_space=pl.ANY)],
        out_specs=pl.BlockSpec((ns, rows, KV_RANK), lambda i, *_: (i, 0, 0)),
        scratch_shapes=[pltpu.VMEM((2, ns, past, KV_RANK), F32), pltpu.VMEM((2, ns, MLA_ROPE, past), F32),
                        pltpu.SemaphoreType.DMA((2, 2))],
    )
    return pl.pallas_call(
        functools.partial(_attn_sample_kernel, n_pages=n_pages, page=page, t_new=t_new),
        grid_spec=grid_spec,
        out_shape=jax.ShapeDtypeStruct((n_seq, rows, KV_RANK), BF16),
        compiler_params=_cparams(("arbitrary",)),
        name="attn_sample",
    )(layer, page_table, q, k_new, v_new, cache_ckv, cache_kpe_t)


def _merge_kernel(layer_ref, x_ref, yp_ref, oh_ref, ol_ref, wg_ref, wuv_ref, wp_ref, wh_ref, wm_ref, wo_ref,
                  g_ref, b_ref, y_ref, *, alpha):
    del layer_ref
    dot = functools.partial(jnp.dot, preferred_element_type=F32)
    x = x_ref[...]
    xb = x.astype(BF16)

    def gate(i):
        return _sigmoid(dot(xb, wg_ref[:, i * D_MODEL:(i + 1) * D_MODEL]))

    o_mla = dot(ol_ref[...], wuv_ref[...]).astype(BF16)
    merged = (gate(0) * dot(yp_ref[...], wp_ref[...]) + gate(1) * dot(oh_ref[...], wh_ref[...])
              + gate(2) * dot(o_mla, wm_ref[...]))
    h = alpha * x + dot(merged.astype(BF16), wo_ref[...])
    y_ref[...] = _layer_norm(h, g_ref[...], b_ref[...])


def _merge(layer, x, y_pool, o_hg, o_lat, wg, wuv, wp, wh, wm, wo, g, b, alpha, tm):
    n = x.shape[0]
    row = lambda w: pl.BlockSpec((tm, w), lambda i, *_: (i, 0))
    consts = (wg, wuv, wp, wh, wm, wo, g, b)
    return pl.pallas_call(
        functools.partial(_merge_kernel, alpha=alpha),
        grid_spec=_layer_grid((n // tm,),
                              [row(D_MODEL), row(D_BR), row(D_BR), row(MLA_HEADS * KV_RANK)]
                              + [_layer_spec(a) for a in consts], row(D_MODEL)),
        out_shape=jax.ShapeDtypeStruct((n, D_MODEL), F32),
        compiler_params=_cparams(("arbitrary",)),
        name="merge",
    )(layer, x, y_pool, o_hg, o_lat, *consts)


def _ffn_kernel(layer_ref, x_ref, wup_ref, cw_ref, cb_ref, wdn_ref, g_ref, b_ref, *rest, alpha, tm, seq_rows):
    del layer_ref
    long_seq = seq_rows % tm == 0
    if long_seq:
        y_ref, tail_ref, halo_ref, act_ref = rest
    else:
        st_ref, y_ref, tail_ref, halo_ref, act_ref = rest
    x = x_ref[...]
    xb = x.astype(BF16)
    sub = V7X_SUBLANES
    if long_seq:
        @pl.when(pl.program_id(0) % (seq_rows // tm) == 0)
        def _():
            halo_ref[...] = jnp.zeros(halo_ref.shape, F32)

        r8 = lax.broadcasted_iota(jnp.int32, (sub, 1), 0)
    else:
        n_s = tm // seq_rows
        t = lax.broadcasted_iota(jnp.int32, (tm, 1), 0) % seq_rows

    def conv_cols(lo):
        cs = slice(lo, lo + FFN_COL_BLOCK)
        up = jnp.dot(xb, wup_ref[:, cs], preferred_element_type=F32)
        prev1 = pltpu.roll(up, 1, 0)
        prev2 = pltpu.roll(up, 2, 0)
        if long_seq:
            last = halo_ref[sub - 1:sub, cs]
            last2 = halo_ref[sub - 2:sub - 1, cs]
            head1 = jnp.where(r8 == 0, last, prev1[:sub])
            head2 = jnp.where(r8 == 0, last2, jnp.where(r8 == 1, last, prev2[:sub]))
            prev1 = jnp.concatenate([head1, prev1[sub:]], axis=0)
            prev2 = jnp.concatenate([head2, prev2[sub:]], axis=0)
            halo_ref[:, cs] = up[tm - sub:, :]
            tail_ref[0, :, cs] = up[tm - sub:, :]
        else:
            st = st_ref[:, :, cs]
            wide = (n_s, seq_rows, FFN_COL_BLOCK)
            last = jnp.broadcast_to(st[:, 1:2, :], wide).reshape(tm, FFN_COL_BLOCK)
            last2 = jnp.broadcast_to(st[:, 0:1, :], wide).reshape(tm, FFN_COL_BLOCK)
            prev1 = jnp.where(t == 0, last, prev1)
            prev2 = jnp.where(t == 0, last2, jnp.where(t == 1, last, prev2))
            tail_ref[:, cs] = up
        return cb_ref[:, cs] + prev2 * cw_ref[0:1, cs] + prev1 * cw_ref[1:2, cs] + up * cw_ref[2:3, cs]

    for jb in range(D_FF // FFN_COL_BLOCK):
        lo = jb * FFN_COL_BLOCK
        act_ref[:, lo:lo + FFN_COL_BLOCK] = (_silu(conv_cols(lo)) * conv_cols(D_FF + lo)).astype(BF16)
    f = jnp.dot(act_ref[...], wdn_ref[...], preferred_element_type=F32)
    y_ref[...] = _layer_norm(alpha * x + f, g_ref[...], b_ref[...])


def _ffn(layer, x, wup, cw, cb, wdn, g, b, states, alpha, tm, seq_rows):
    n = x.shape[0]
    row = pl.BlockSpec((tm, D_MODEL), lambda i, *_: (i, 0))
    long_seq = seq_rows % tm == 0
    assert long_seq == (states is None)
    if long_seq:
        st_in = []
        tail_spec = pl.BlockSpec((1, V7X_SUBLANES, 2 * D_FF), lambda i, *_: (i, 0, 0))
        tail_shape = jax.ShapeDtypeStruct((n // tm, V7X_SUBLANES, 2 * D_FF), F32)
    else:
        n_s = tm // seq_rows
        st_in = [pl.BlockSpec((None, n_s, CONV_W - 1, 2 * D_FF), lambda i, lyr: (lyr[0], i, 0, 0))]
        tail_spec = pl.BlockSpec((tm, 2 * D_FF), lambda i, *_: (i, 0))
        tail_shape = jax.ShapeDtypeStruct((n, 2 * D_FF), F32)
    consts = (wup, cw, cb, wdn, g, b)
    return pl.pallas_call(
        functools.partial(_ffn_kernel, alpha=alpha, tm=tm, seq_rows=seq_rows),
        grid_spec=_layer_grid((n // tm,), [row] + [_layer_spec(a) for a in consts] + st_in,
                              [row, tail_spec],
                              [pltpu.VMEM((V7X_SUBLANES, 2 * D_FF), F32), pltpu.VMEM((tm, D_FF), BF16)]),
        out_shape=[jax.ShapeDtypeStruct((n, D_MODEL), F32), tail_shape],
        compiler_params=_cparams(("arbitrary",)),
        name="ffn",
    )(layer, x, *consts, *([] if long_seq else [states]))


def _pack_weights(w_in, w_uq, w_uk, w_uv):
    depth = w_in.shape[0]
    x1 = w_in[:, :, C_KPE_SRC:C_KPE_SRC + ROPE_HALF]
    x2 = w_in[:, :, C_KPE_SRC + ROPE_HALF:C_GATE_SRC]
    pad = jnp.zeros((depth, D_MODEL, V7X_LANES - MLA_ROPE), w_in.dtype)
    wa = jnp.concatenate([w_in[:, :, :C_KPE_SRC], x1, x2, pad, x2, x1, pad], axis=-1).astype(BF16)
    wg = w_in[:, :, C_GATE_SRC:].astype(BF16)
    uq = w_uq.reshape(depth, Q_RANK, MLA_HEADS, MLA_NOPE + MLA_ROPE)
    wq = jnp.concatenate([
        uq[..., :MLA_NOPE].reshape(depth, Q_RANK, -1),
        uq[..., MLA_NOPE:MLA_NOPE + ROPE_HALF].reshape(depth, Q_RANK, -1),
        uq[..., MLA_NOPE + ROPE_HALF:].reshape(depth, Q_RANK, -1)], axis=-1).astype(BF16)
    eye = jnp.eye(MLA_HEADS, dtype=w_uk.dtype)
    uk = jnp.transpose(w_uk, (0, 2, 3, 1))
    wuk = (eye[None, :, None, :, None] * uk[:, :, :, None, :]).reshape(
        depth, MLA_HEADS * MLA_NOPE, MLA_HEADS * KV_RANK).astype(BF16)
    uv = jnp.transpose(w_uv, (0, 2, 1, 3))
    wuv = (eye[None, :, None, :, None] * uv[:, :, :, None, :]).reshape(
        depth, MLA_HEADS * KV_RANK, MLA_HEADS * MLA_V).astype(BF16)
    return wa, wg, wq, wuk, wuv


def _rope_tables(pos):
    inv = ROPE_THETA ** (-jnp.arange(ROPE_HALF, dtype=F32) / ROPE_HALF)
    ang = pos.astype(F32)[:, None] * inv[None, :]
    reps = V7X_LANES // ROPE_HALF
    return jnp.tile(jnp.cos(ang), (1, reps)), jnp.tile(jnp.sin(ang), (1, reps))


def _heads_q(q_lat, q_pe):
    n = q_lat.shape[0]
    pe = q_pe.reshape(n, 2, MLA_HEADS, ROPE_HALF).transpose(0, 2, 1, 3).reshape(n, MLA_HEADS, MLA_ROPE)
    return jnp.concatenate([q_lat.reshape(n, MLA_HEADS, KV_RANK), pe], axis=-1).reshape(n * MLA_HEADS, D_QK)


def _tiles(n_rows, short_seqs):
    return dict(in_proj=min(512, n_rows), merge=min(512, n_rows),
                ffn=min(256 if short_seqs else 512, n_rows))


FFN_COL_BLOCK = 256


POOL_TILE = 512
HGRN_TILE = 256
HGRN_CHUNK = 32
HGRN_SHORT_TILE = 128
ATTN_TQ = 512
ATTN_TK = 512
SAMPLE_SEQS_PER_STEP = 1


def _trunk(x_prompt, x_sample, cache_ckv, cache_kpe, state_pool, state_hgrn, state_conv, page_table,
           w_in, pool_w, pool_scale, hg_lower_bounds, hg_norm, q_norm, w_uq, kv_norm, w_uk, w_uv,
           w_br_pool, w_br_hg, w_br_mla, w_out, ln1_g, ln1_b, ffn_w_up, ffn_conv_w, ffn_conv_b,
           ffn_w_down, ln2_g, ln2_b):
    depth = w_in.shape[0]
    b, t, _ = x_prompt.shape
    sb, ts, _ = x_sample.shape
    past_len = page_table.shape[1] * cache_ckv.shape[2]
    alpha = (2 * depth) ** 0.25

    wa, wg, wq, wuk, wuv = _pack_weights(w_in, w_uq, w_uk, w_uv)
    bf = lambda a: a.astype(BF16)
    pool_wb, wp, wh, wm, wo, wup, wdn = map(bf, (pool_w, w_br_pool, w_br_hg, w_br_mla, w_out, ffn_w_up,
                                                  ffn_w_down))
    row2 = lambda a: a.reshape(depth, 1, -1)
    pool_scale, hg_norm, q_norm, kv_norm, ln1_g, ln1_b, ln2_g, ln2_b, ffn_conv_b = map(
        row2, (pool_scale, hg_norm, q_norm, kv_norm, ln1_g, ln1_b, ln2_g, ln2_b, ffn_conv_b))

    np_, ns_ = b * t, sb * ts
    tp, tsm = _tiles(np_, False), _tiles(ns_, True)
    cos_p, sin_p = _rope_tables(jnp.arange(t, dtype=jnp.int32))
    cos_s, sin_s = _rope_tables(past_len + jnp.arange(tsm["in_proj"], dtype=jnp.int32) % ts)

    xp = x_prompt.reshape(np_, D_MODEL)
    xs = x_sample.reshape(ns_, D_MODEL)
    cache_kpe_t = jnp.swapaxes(cache_kpe, 2, 3)
    ones_rows = jnp.zeros((b, VT_ROWS - KV_RANK, t), BF16).at[:, 0, :].set(1.0)
    outs_p = [[] for _ in range(5)]
    outs_s = [[] for _ in range(5)]
    proj_w = (wa, hg_lower_bounds, q_norm, kv_norm, wq, wuk)
    merge_w = (wg, wuv, wp, wh, wm, wo, ln1_g, ln1_b)
    ffn_w = (wup, ffn_conv_w, ffn_conv_b, wdn, ln2_g, ln2_b)

    for l in range(depth):
        layer = jnp.full((1,), l, jnp.int32)

        (u, qa, kk, lf, vv, sg, q_lat, q_pe, ckv, kpe, kcat) = _in_proj(
            layer, xp, *proj_w, cos_p, sin_p, tp["in_proj"])
        y_pool = _pool_prompt(layer, u.reshape(b, t, D_BR), pool_wb, pool_scale, min(POOL_TILE, t))
        o_hg, s_fin = _hgrn(layer, qa, kk, lf, vv, sg, hg_norm, None, b, t, min(HGRN_TILE, t),
                            min(HGRN_CHUNK, t))
        kcat = kcat.reshape(b, t, D_QK)
        vt = jnp.concatenate([jnp.swapaxes(kcat[:, :, :KV_RANK], 1, 2), ones_rows], axis=1)
        o_lat = _attn_prompt(q_lat, q_pe, kcat, vt, min(ATTN_TQ, t), min(ATTN_TK, t))
        x1 = _merge(layer, xp, y_pool.reshape(np_, D_BR), o_hg, o_lat, *merge_w, alpha, tp["merge"])
        xp, tail = _ffn(layer, x1, *ffn_w, None, alpha, tp["ffn"], t)
        tiles_per_seq = t // tp["ffn"]
        conv_p = tail.reshape(b, tiles_per_seq, V7X_SUBLANES, 2 * D_FF)[:, -1, -(CONV_W - 1):, :]
        for lst, v in zip(outs_p, (ckv.reshape(b, t, KV_RANK), kpe.reshape(b, t, MLA_ROPE),
                                   u.reshape(b, t, D_BR)[:, -POOL_BUF:], s_fin, conv_p)):
            lst.append(v)

        (u, qa, kk, lf, vv, sg, q_lat, q_pe, ckv, kpe, _) = _in_proj(
            layer, xs, *proj_w, cos_s, sin_s, tsm["in_proj"])
        ext = jnp.concatenate([state_pool[l], u.reshape(sb, ts, D_BR)], axis=1)
        y_pool = _pool_sample(layer, jnp.swapaxes(ext, 0, 1), pool_wb, pool_scale, ts, past_len)
        y_pool = jnp.swapaxes(y_pool, 0, 1).reshape(ns_, D_BR)
        o_hg, s_fin = _hgrn(layer, qa, kk, lf, vv, sg, hg_norm, state_hgrn, sb, ts,
                            min(HGRN_SHORT_TILE, ns_), ts)
        k_new = jnp.concatenate([ckv, kpe], axis=-1).reshape(sb, ts, D_QK)
        o_lat = _attn_sample(layer, page_table, _heads_q(q_lat, q_pe).reshape(sb, ts * MLA_HEADS, D_QK),
                             k_new, ckv.reshape(sb, ts, KV_RANK), cache_ckv, cache_kpe_t)
        x1 = _merge(layer, xs, y_pool, o_hg, o_lat.reshape(ns_, MLA_HEADS * KV_RANK), *merge_w, alpha,
                    tsm["merge"])
        xs, up = _ffn(layer, x1, *ffn_w, state_conv, alpha, tsm["ffn"], ts)
        conv_s = jnp.concatenate([state_conv[l], up.reshape(sb, ts, 2 * D_FF)], axis=1)[:, -(CONV_W - 1):]
        for lst, v in zip(outs_s, (ckv.reshape(sb, ts, KV_RANK), kpe.reshape(sb, ts, MLA_ROPE),
                                   ext[:, -POOL_BUF:], s_fin, conv_s)):
            lst.append(v)

    stack = lambda lists: [jnp.stack(v) for v in lists]
    return (xp.reshape(b, t, D_MODEL), xs.reshape(sb, ts, D_MODEL), *stack(outs_p), *stack(outs_s))


def kernel(x_prompt, x_sample, cache_ckv, cache_kpe, state_pool, state_hgrn, state_conv, page_table, w_in, pool_w, pool_scale, hg_lower_bounds, hg_norm, q_norm, w_uq, kv_norm, w_uk, w_uv, w_br_pool, w_br_hg, w_br_mla, w_out, ln1_g, ln1_b, ffn_w_up, ffn_conv_w, ffn_conv_b, ffn_w_down, ln2_g, ln2_b):
    return _trunk(x_prompt, x_sample, cache_ckv, cache_kpe, state_pool, state_hgrn, state_conv, page_table,
                  w_in, pool_w, pool_scale, hg_lower_bounds, hg_norm, q_norm, w_uq, kv_norm, w_uk, w_uv,
                  w_br_pool, w_br_hg, w_br_mla, w_out, ln1_g, ln1_b, ffn_w_up, ffn_conv_w, ffn_conv_b,
                  ffn_w_down, ln2_g, ln2_b)
```
